```python
import jax, jax.numpy as jnp
from jax import lax
import numpy as np

D_MODEL = 1024
BATCH = 8
SEQ = 4096
DEPTH = 4

N_MIXERS = 2
N_HGRN_LAYERS = (DEPTH + N_MIXERS - 1) // N_MIXERS
N_ATTN_LAYERS = (DEPTH - 1 + N_MIXERS - 1) // N_MIXERS

HGRN_HEAD_DIM = 128
HGRN_HEADS = D_MODEL // HGRN_HEAD_DIM
HGRN_CHUNK = 64

ATTN_HEADS = 16
ATTN_HEAD_DIM = D_MODEL // ATTN_HEADS
DILATED_GROUPS = ((128, 1), (512, 4), (2048, 16))
N_GROUPS = len(DILATED_GROUPS)
ATTN_BLOCK = 128

MLP_HIDDEN = 4 * D_MODEL
NORM_EPS = 1e-6

kernel_name = "hybrid_hgrn2_dilated_attn_trunk"


def _rmsnorm(x, gain):
    xf = x.astype(jnp.float32)
    y = xf * lax.rsqrt(jnp.mean(xf * xf, axis=-1, keepdims=True) + NORM_EPS)
    return (y * gain.astype(jnp.float32)).astype(x.dtype)


def _hgrn2_chunk_scan(q, k, v, log_f):
    b, s, h, dk = q.shape
    dv = v.shape[-1]
    nc = s // HGRN_CHUNK

    def to_chunks(t):
        return t.reshape(b, nc, HGRN_CHUNK, h, t.shape[-1]).transpose(1, 0, 3, 2, 4)

    causal = jnp.tril(jnp.ones((HGRN_CHUNK, HGRN_CHUNK), dtype=bool))

    def step(state, inp):
        qi, ki, vi, gi = inp
        cum = jnp.cumsum(gi, axis=2)
        o_inter = jnp.einsum('bhtk,bhkv->bhtv', qi * jnp.exp(cum), state)
        diff = cum[:, :, :, None, :] - cum[:, :, None, :, :]
        decay = jnp.exp(jnp.where(causal[:, :, None], diff, -jnp.inf))
        scores = jnp.einsum('bhtk,bhtsk,bhsk->bhts', qi, decay, ki)
        o_intra = jnp.einsum('bhts,bhsv->bhtv', scores, vi)
        last = cum[:, :, -1:, :]
        k_dec = ki * jnp.exp(last - cum)
        new_state = (jnp.exp(last[:, :, 0, :])[..., None] * state
                     + jnp.einsum('bhsk,bhsv->bhkv', k_dec, vi))
        return new_state, o_inter + o_intra

    init = jnp.zeros((b, h, dk, dv), jnp.float32)
    _, o = lax.scan(step, init, (to_chunks(q), to_chunks(k), to_chunks(v), to_chunks(log_f)))
    return o.transpose(1, 0, 3, 2, 4).reshape(b, s, h, dv)


def _hgrn2_mixer(h, w_in, lower_bound, out_norm, w_out):
    b, s, _ = h.shape
    proj = h @ w_in
    q, f, i, g = jnp.split(proj, 4, axis=-1)
    ff = f.astype(jnp.float32)
    lb = lower_bound.astype(jnp.float32)
    log_f = jnp.logaddexp(jnp.log(lb), jnp.log1p(-lb) + jax.nn.log_sigmoid(ff))
    k = (1.0 - lb) * jax.nn.sigmoid(-ff)
    qf = jax.nn.silu(q.astype(jnp.float32)) * (HGRN_HEAD_DIM ** -0.5)
    heads = lambda t: t.reshape(b, s, HGRN_HEADS, HGRN_HEAD_DIM)
    o = _hgrn2_chunk_scan(heads(qf), heads(k), heads(i.astype(jnp.float32)), heads(log_f))
    o = o * lax.rsqrt(jnp.mean(o * o, axis=-1, keepdims=True) + NORM_EPS)
    o = o.reshape(b, s, D_MODEL) * out_norm.astype(jnp.float32) * jax.nn.silu(g.astype(jnp.float32))
    return o.astype(h.dtype) @ w_out


def _banded_causal_attention(q, k, v, reach):
    n, l, h, dh = q.shape
    blk = ATTN_BLOCK
    nb = -(-l // blk)
    lp = nb * blk
    qb = jnp.pad(q, ((0, 0), (0, lp - l), (0, 0), (0, 0))).reshape(n, nb, blk, h, dh)
    pad_kv = ((0, 0), (blk, lp - l), (0, 0), (0, 0))
    kp = jnp.pad(k, pad_kv).reshape(n, nb + 1, blk, h, dh)
    vp = jnp.pad(v, pad_kv).reshape(n, nb + 1, blk, h, dh)
    kb = jnp.concatenate([kp[:, :-1], kp[:, 1:]], axis=2)
    vb = jnp.concatenate([vp[:, :-1], vp[:, 1:]], axis=2)
    bidx = jnp.arange(nb)[:, None, None]
    qpos = bidx * blk + jnp.arange(blk)[None, :, None]
    kpos = (bidx - 1) * blk + jnp.arange(2 * blk)[None, None, :]
    dist = qpos - kpos
    mask = (kpos >= 0) & (dist >= 0) & (dist <= reach)
    s = jnp.einsum('nbqhd,nbkhd->nbhqk', qb, kb).astype(jnp.float32) * (dh ** -0.5)
    s = jnp.where(mask[None, :, None], s, -jnp.inf)
    lse = jax.nn.logsumexp(s, axis=-1)
    p = jnp.exp(s - lse[..., None])
    o = jnp.einsum('nbhqk,nbkhd->nbqhd', p.astype(v.dtype), vb).astype(jnp.float32)
    o = o.reshape(n, lp, h, dh)[:, :l]
    lse = lse.transpose(0, 1, 3, 2).reshape(n, lp, h)[:, :l]
    return o, lse


def _dilated_group(q, k, v, dilation, reach):
    b, s, h, dh = q.shape
    l = s // dilation

    def split(t):
        return t.reshape(b, l, dilation, h, dh).transpose(0, 2, 1, 3, 4).reshape(b * dilation, l, h, dh)

    o, lse = _banded_causal_attention(split(q), split(k), split(v), reach)
    o = o.reshape(b, dilation, l, h, dh).transpose(0, 2, 1, 3, 4).reshape(b, s, h, dh)
    lse = lse.reshape(b, dilation, l, h).transpose(0, 2, 1, 3).reshape(b, s, h)
    return o, lse


def _dilated_attention_mixer(h, w_qkv, w_out):
    b, s, _ = h.shape
    proj = (h @ w_qkv).reshape(b, s, N_GROUPS, 3, ATTN_HEADS, ATTN_HEAD_DIM)
    outs, lses = [], []
    for g, (window, dilation) in enumerate(DILATED_GROUPS):
        o, lse = _dilated_group(proj[:, :, g, 0], proj[:, :, g, 1], proj[:, :, g, 2],
                                dilation, window // dilation)
        outs.append(o)
        lses.append(lse)
    weights = jax.nn.softmax(jnp.stack(lses, axis=0), axis=0)
    o = jnp.sum(weights[..., None] * jnp.stack(outs, axis=0), axis=0)
    return o.reshape(b, s, D_MODEL).astype(h.dtype) @ w_out


def _squared_relu_mlp(h, w1, w2):
    a = jax.nn.relu(h @ w1)
    return (a * a) @ w2


def setup_inputs(seed: int = 0) -> dict:
    key = jax.random.key(seed)
    ks = jax.random.split(key, 13)
    nrm = lambda k, shape: jax.random.normal(k, shape, jnp.float32)
    d = D_MODEL
    return {
        "x": nrm(ks[0], (BATCH, SEQ, d)),
        "norm_mix": 1.0 + 0.02 * nrm(ks[1], (DEPTH, d)),
        "norm_mlp": 1.0 + 0.02 * nrm(ks[2], (DEPTH, d)),
        "norm_final": 1.0 + 0.02 * nrm(ks[3], (d,)),
        "hgrn_w_in": nrm(ks[4], (N_HGRN_LAYERS, d, 4 * d)) * d ** -0.5,
        "hgrn_lower_bound": 0.5 * nrm(ks[5], (N_HGRN_LAYERS, d)),
        "hgrn_out_norm": 1.0 + 0.02 * nrm(ks[6], (N_HGRN_LAYERS, d)),
        "hgrn_w_out": nrm(ks[7], (N_HGRN_LAYERS, d, d)) * d ** -0.5,
        "attn_w_qkv": nrm(ks[8], (N_ATTN_LAYERS, d, N_GROUPS * 3 * d)) * d ** -0.5,
        "attn_w_out": nrm(ks[9], (N_ATTN_LAYERS, d, d)) * d ** -0.5,
        "mlp_w1": nrm(ks[10], (DEPTH, d, MLP_HIDDEN)) * d ** -0.5,
        "mlp_w2": nrm(ks[11], (DEPTH, MLP_HIDDEN, d)) * MLP_HIDDEN ** -0.5,
    }


def reference(x, norm_mix, norm_mlp, norm_final, hgrn_w_in, hgrn_lower_bound, hgrn_out_norm,
              hgrn_w_out, attn_w_qkv, attn_w_out, mlp_w1, mlp_w2):
    lb_all = jnp.cumsum(jax.nn.softmax(hgrn_lower_bound.astype(jnp.float32), axis=0), axis=0)
    lb_all = lb_all - lb_all[0]
    for layer in range(DEPTH):
        j = layer // N_MIXERS
        h = _rmsnorm(x, norm_mix[layer])
        if layer % N_MIXERS == 0:
            mix = _hgrn2_mixer(h, hgrn_w_in[j], lb_all[j], hgrn_out_norm[j], hgrn_w_out[j])
        else:
            mix = _dilated_attention_mixer(h, attn_w_qkv[j], attn_w_out[j])
        x = x + mix
        x = x + _squared_relu_mlp(_rmsnorm(x, norm_mlp[layer]), mlp_w1[layer], mlp_w2[layer])
    return _rmsnorm(x, norm_final)
```

```python
import functools

import numpy as np
import jax
import jax.numpy as jnp
from jax import lax
from jax.experimental import pallas as pl
from jax.experimental.pallas import tpu as pltpu

F32 = jnp.float32
BF16 = jnp.bfloat16

D_MODEL = 1024
NORM_EPS = 1e-6
MLP_HIDDEN = 4 * D_MODEL

HGRN_HEAD_DIM = 128
HGRN_HEADS = D_MODEL // HGRN_HEAD_DIM
HGRN_CHUNK = 128

ATTN_HEADS = 16
ATTN_HEAD_DIM = D_MODEL // ATTN_HEADS
DILATED_GROUPS = ((128, 1), (512, 4), (2048, 16))
N_GROUPS = len(DILATED_GROUPS)
ATTN_BLOCK = 128
LSE_LANES = 128

VMEM_LIMIT_BYTES = 56 * 1024 * 1024

ROW_TILE = 512


def _dot(a, b):
    return jnp.dot(a, b, preferred_element_type=F32)


def _dot_nt(a, b):
    return lax.dot_general(a, b, (((1,), (1,)), ((), ())), preferred_element_type=F32)


def _dot_tn(a, b):
    return lax.dot_general(a, b, (((0,), (0,)), ((), ())), preferred_element_type=F32)


def _rmsnorm_rows(x, gain):
    return x * lax.rsqrt(jnp.mean(x * x, axis=-1, keepdims=True) + NORM_EPS) * gain


def _params(*semantics):
    return pltpu.CompilerParams(dimension_semantics=semantics, vmem_limit_bytes=VMEM_LIMIT_BYTES)


def _const_spec(shape):
    zeros = (0,) * len(shape)
    return pl.BlockSpec(shape, lambda *_: zeros)


def _hgrn_pre_kernel(layer_j, x_ref, gain_ref, w_ref, lbp_ref, q_ref, k_ref, v_ref, lf_ref, sg_ref):
    d = D_MODEL
    h = _rmsnorm_rows(x_ref[...], gain_ref[...]).astype(BF16)

    p = lbp_ref[...]
    e = jnp.exp(p - jnp.max(p, axis=0, keepdims=True))
    sm = e / jnp.sum(e, axis=0, keepdims=True)
    lb = jnp.zeros((1, d), F32)
    for i in range(1, layer_j + 1):
        lb = lb + sm[i:i + 1, :]
    log_lb = jnp.log(lb)
    log_1m_lb = jnp.log1p(-lb)

    q = _dot(h, w_ref[:, 0:d])
    q_ref[...] = (q * (HGRN_HEAD_DIM ** -0.5) / (1.0 + jnp.exp(-q))).astype(BF16)

    f = _dot(h, w_ref[:, d:2 * d])
    ef = jnp.exp(-jnp.abs(f))
    inv = 1.0 / (1.0 + ef)
    log_sig = jnp.minimum(f, 0.0) - jnp.log1p(ef)
    b = log_1m_lb + log_sig
    lf_ref[...] = jnp.maximum(log_lb, b) + jnp.log1p(jnp.exp(-jnp.abs(log_lb - b)))
    sig_neg = jnp.where(f >= 0.0, ef * inv, inv)
    k_ref[...] = ((1.0 - lb) * sig_neg).astype(BF16)

    v_ref[...] = _dot(h, w_ref[:, 2 * d:3 * d]).astype(BF16)

    g = _dot(h, w_ref[:, 3 * d:4 * d])
    sg_ref[...] = (g / (1.0 + jnp.exp(-g))).astype(BF16)


def _hgrn_pre(x2, gain, w_in, lbp, layer_j):
    t, d = x2.shape
    tm = min(ROW_TILE, t)
    row = lambda i: (i, 0)
    out_bf = jax.ShapeDtypeStruct((t, d), BF16)
    return pl.pallas_call(
        functools.partial(_hgrn_pre_kernel, layer_j),
        grid=(t // tm,),
        in_specs=[pl.BlockSpec((tm, d), row), _const_spec((1, d)), _const_spec((d, 4 * d)),
                  _const_spec(lbp.shape)],
        out_specs=[pl.BlockSpec((tm, d), row)] * 5,
        out_shape=[out_bf, out_bf, out_bf, jax.ShapeDtypeStruct((t, d), F32), out_bf],
        compiler_params=_params("arbitrary"),
        name="hgrn_pre",
    )(x2, gain, w_in, lbp)


def _hgrn_levels(c):
    m, out = 1, []
    while m < c:
        out.append(m)
        m *= 2
    return out


def _hgrn_tables(c):
    levels = _hgrn_levels(c)
    idx = np.arange(c)
    x, j = idx[:, None], idx[None, :]
    blocks = [(j <= x), (j > x)]
    masks = [(x == j)]
    for m in levels:
        bnd = (x // (2 * m)) * (2 * m) + m - 1
        second = (x % (2 * m)) >= m
        blocks.append(np.where(second, (j > bnd) & (j <= x), (j > x) & (j <= bnd)))
        t, s = x, j
        masks.append((t // (2 * m) == s // (2 * m)) & ((t % (2 * m)) >= m) & ((s % (2 * m)) < m))
    a = np.concatenate(blocks, axis=0).astype(np.float32)
    a2 = np.concatenate([a, a], axis=1)
    return jnp.asarray(a2, BF16), jnp.asarray(np.stack(masks).astype(np.float32))


def _hgrn_scan_kernel(q_ref, k_ref, v_ref, lf_ref, sg_ref, a2_ref, mask_ref, onorm_ref, y_ref,
                      state_ref, g_ref):
    c = HGRN_CHUNK
    hd = HGRN_HEAD_DIM
    n_levels = len(_hgrn_levels(c))

    @pl.when(pl.program_id(1) == 0)
    def _():
        state_ref[...] = jnp.zeros_like(state_ref)

    lf = lf_ref[0]
    lf_hi = lf.astype(BF16)
    lf_lo = (lf - lf_hi.astype(F32)).astype(BF16)
    g_ref[...] = _dot(a2_ref[...], jnp.concatenate([lf_hi, lf_lo], axis=0))

    for h in range(HGRN_HEADS):
        sl = slice(h * hd, (h + 1) * hd)
        q_bf = q_ref[0, :, sl]
        k_bf = k_ref[0, :, sl]
        v_bf = v_ref[0, :, sl]
        q = q_bf.astype(F32)
        k = k_bf.astype(F32)
        cum = g_ref[0:c, sl]
        st = state_ref[h]

        o = _dot_nt((q * jnp.exp(cum)).astype(BF16), st.astype(BF16))
        scores = mask_ref[0] * _dot_nt(q_bf, k_bf)
        for li in range(n_levels):
            e = jnp.exp(g_ref[(2 + li) * c:(3 + li) * c, sl])
            scores = scores + mask_ref[1 + li] * _dot_nt((q * e).astype(BF16), (k * e).astype(BF16))
        o = o + _dot(scores.astype(BF16), v_bf)

        k_dec = (k * jnp.exp(g_ref[c:2 * c, sl])).astype(BF16)
        state_ref[h] = st * jnp.exp(cum[c - 1:c, :]) + _dot_tn(v_bf, k_dec)

        o = o * lax.rsqrt(jnp.mean(o * o, axis=-1, keepdims=True) + NORM_EPS)
        y_ref[0, :, sl] = (o * onorm_ref[:, sl] * sg_ref[0, :, sl].astype(F32)).astype(BF16)


def _hgrn_scan(q, k, v, lf, sg, out_norm):
    b, s, d = q.shape
    c = HGRN_CHUNK
    a2, masks = _hgrn_tables(c)
    tok = pl.BlockSpec((1, c, d), lambda i, j: (i, j, 0))
    return pl.pallas_call(
        _hgrn_scan_kernel,
        grid=(b, s // c),
        in_specs=[tok, tok, tok, tok, tok, _const_spec(a2.shape), _const_spec(masks.shape),
                  _const_spec((1, d))],
        out_specs=tok,
        out_shape=jax.ShapeDtypeStruct((b, s, d), BF16),
        scratch_shapes=[pltpu.VMEM((HGRN_HEADS, HGRN_HEAD_DIM, HGRN_HEAD_DIM), F32),
                        pltpu.VMEM((a2.shape[0], d), F32)],
        compiler_params=_params("arbitrary", "arbitrary"),
        name="hgrn_scan",
    )(q, k, v, lf, sg, a2, masks, out_norm)


def _norm_matmul_kernel(x_ref, gain_ref, w_ref, o_ref, h_ref):
    @pl.when(pl.program_id(1) == 0)
    def _():
        h_ref[...] = _rmsnorm_rows(x_ref[...], gain_ref[...]).astype(BF16)

    o_ref[...] = _dot(h_ref[...], w_ref[...]).astype(o_ref.dtype)


def _norm_matmul(x2, gain, w, tn):
    t, d = x2.shape
    n = w.shape[1]
    tm = min(ROW_TILE, t)
    return pl.pallas_call(
        _norm_matmul_kernel,
        grid=(t // tm, n // tn),
        in_specs=[pl.BlockSpec((tm, d), lambda i, j: (i, 0)), _const_spec((1, d)),
                  pl.BlockSpec((d, tn), lambda i, j: (0, j))],
        out_specs=pl.BlockSpec((tm, tn), lambda i, j: (i, j)),
        out_shape=jax.ShapeDtypeStruct((t, n), BF16),
        scratch_shapes=[pltpu.VMEM((tm, d), BF16)],
        compiler_params=_params("arbitrary", "arbitrary"),
        name="attn_qkv",
    )(x2, gain, w)


def _attn_group_kernel(q_ref, kp_ref, kc_ref, vp_ref, vc_ref, o_ref, lse_ref):
    blk = ATTN_BLOCK
    dh = ATTN_HEAD_DIM
    has_prev = pl.program_id(2) > 0
    row = lax.broadcasted_iota(jnp.int32, (blk, blk), 0)
    col = lax.broadcasted_iota(jnp.int32, (blk, blk), 1)
    prev_ok = (col >= row) & has_prev
    cur_ok = col <= row
    lane = lax.broadcasted_iota(jnp.int32, (blk, LSE_LANES), 1)
    lse_tile = jnp.zeros((blk, LSE_LANES), F32)
    neg_inf = F32(-jnp.inf)

    for h in range(ATTN_HEADS):
        sl = slice(h * dh, (h + 1) * dh)
        qh = q_ref[0, :, sl] * BF16(dh ** -0.5)
        s_prev = jnp.where(prev_ok, _dot_nt(qh, kp_ref[0, :, sl]), neg_inf)
        s_cur = jnp.where(cur_ok, _dot_nt(qh, kc_ref[0, :, sl]), neg_inf)
        m = jnp.maximum(jnp.max(s_prev, axis=-1, keepdims=True), jnp.max(s_cur, axis=-1, keepdims=True))
        p_prev = jnp.exp(s_prev - m)
        p_cur = jnp.exp(s_cur - m)
        denom = jnp.sum(p_prev, axis=-1, keepdims=True) + jnp.sum(p_cur, axis=-1, keepdims=True)
        inv = 1.0 / denom
        o = _dot((p_prev * inv).astype(BF16), vp_ref[0, :, sl]) + _dot((p_cur * inv).astype(BF16), vc_ref[0, :, sl])
        o_ref[0, :, sl] = o.astype(o_ref.dtype)
        lse_tile = jnp.where(lane == h, m + jnp.log(denom), lse_tile)

    lse_ref[0] = lse_tile


def _attn_group(proj, g, dilation):
    b, s, n = proj.shape
    d = D_MODEL
    l = s // dilation
    blk = ATTN_BLOCK
    ncol = n // d
    view = proj.reshape(b, l, dilation * n)
    base = 3 * g

    def spec(which, prev):
        if prev:
            return pl.BlockSpec((1, blk, d), lambda i, r, j: (i, jnp.maximum(j - 1, 0), r * ncol + base + which))
        return pl.BlockSpec((1, blk, d), lambda i, r, j: (i, j, r * ncol + base + which))

    o, lse = pl.pallas_call(
        _attn_group_kernel,
        grid=(b, dilation, l // blk),
        in_specs=[spec(0, False), spec(1, True), spec(1, False), spec(2, True), spec(2, False)],
        out_specs=[pl.BlockSpec((1, blk, d), lambda i, r, j: (i, j, r)),
                   pl.BlockSpec((1, blk, LSE_LANES), lambda i, r, j: (i, j, r))],
        out_shape=[jax.ShapeDtypeStruct((b, l, dilation * d), BF16),
                   jax.ShapeDtypeStruct((b, l, dilation * LSE_LANES), F32)],
        compiler_params=_params("arbitrary", "arbitrary", "arbitrary"),
        name=f"attn_group{g}",
    )(view, view, view, view, view)
    return o.reshape(b * s, d), lse.reshape(b * s, LSE_LANES)


def _head_expand_matrix():
    e = np.zeros((LSE_LANES, D_MODEL), np.float32)
    for h in range(ATTN_HEADS):
        e[h, h * ATTN_HEAD_DIM:(h + 1) * ATTN_HEAD_DIM] = 1.0
    return jnp.asarray(e, BF16)


def _mlp_tail(x1, gain_ref, w1_ref, w2_ref, final_gain_ref, o_ref):
    d = D_MODEL
    h = _rmsnorm_rows(x1, gain_ref[...]).astype(BF16)
    acc = x1
    for c in range(MLP_HIDDEN // d):
        a = jnp.maximum(_dot(h, w1_ref[:, c * d:(c + 1) * d]), 0.0)
        acc = acc + _dot((a * a).astype(BF16), w2_ref[c * d:(c + 1) * d, :])
    if final_gain_ref is not None:
        acc = _rmsnorm_rows(acc, final_gain_ref[...])
    o_ref[...] = acc


def _post_kernel(final, x_ref, y_ref, wo_ref, gain_ref, w1_ref, w2_ref, *rest):
    if final:
        final_gain_ref, o_ref = rest
    else:
        final_gain_ref, (o_ref,) = None, rest
    x1 = x_ref[...] + _dot(y_ref[...], wo_ref[...])
    _mlp_tail(x1, gain_ref, w1_ref, w2_ref, final_gain_ref, o_ref)


def _attn_post_kernel(final, x_ref, o0_ref, o1_ref, o2_ref, l0_ref, l1_ref, l2_ref, exp_ref, wo_ref,
                      gain_ref, w1_ref, w2_ref, *rest):
    if final:
        final_gain_ref, o_ref = rest
    else:
        final_gain_ref, (o_ref,) = None, rest
    lses = [l0_ref[...], l1_ref[...], l2_ref[...]]
    m = jnp.maximum(jnp.maximum(lses[0], lses[1]), lses[2])
    es = [jnp.exp(l - m) for l in lses]
    inv = 1.0 / (es[0] + es[1] + es[2])
    y = None
    for e, o_g in zip(es, (o0_ref, o1_ref, o2_ref)):
        w = e * inv
        w_hi = w.astype(BF16)
        w_lo = (w - w_hi.astype(F32)).astype(BF16)
        w_full = _dot(w_hi, exp_ref[...]) + _dot(w_lo, exp_ref[...])
        term = w_full * o_g[...].astype(F32)
        y = term if y is None else y + term
    x1 = x_ref[...] + _dot(y.astype(BF16), wo_ref[...])
    _mlp_tail(x1, gain_ref, w1_ref, w2_ref, final_gain_ref, o_ref)


def _post(x2, mix, wo, gain, w1, w2, final_gain):
    t, d = x2.shape
    tm = min(ROW_TILE, t)
    row = lambda i: (i, 0)
    final = final_gain is not None
    weights = [wo, gain, w1, w2] + ([final_gain] if final else [])
    weight_specs = [_const_spec(w.shape) for w in weights]
    if isinstance(mix, tuple):
        expand = _head_expand_matrix()
        body = functools.partial(_attn_post_kernel, final)
        args = [x2, *mix, expand] + weights
        in_specs = ([pl.BlockSpec((tm, d), row)] * 4 + [pl.BlockSpec((tm, LSE_LANES), row)] * 3
                    + [_const_spec(expand.shape)] + weight_specs)
    else:
        body = functools.partial(_post_kernel, final)
        args = [x2, mix] + weights
        in_specs = [pl.BlockSpec((tm, d), row)] * 2 + weight_specs
    return pl.pallas_call(
        body,
        grid=(t // tm,),
        in_specs=in_specs,
        out_specs=pl.BlockSpec((tm, d), row),
        out_shape=jax.ShapeDtypeStruct((t, d), F32),
        compiler_params=_params("arbitrary"),
        name="post_mlp",
    )(*args)


def kernel(x, norm_mix, norm_mlp, norm_final, hgrn_w_in, hgrn_lower_bound, hgrn_out_norm, hgrn_w_out,
           attn_w_qkv, attn_w_out, mlp_w1, mlp_w2):
    b, s, d = x.shape
    depth = norm_mix.shape[0]
    n_mixers = 2
    x2 = x.reshape(b * s, d)
    row = lambda v: v.reshape(1, -1).astype(F32)
    lbp = hgrn_lower_bound.astype(F32)

    for layer in range(depth):
        j = layer // n_mixers
        final_gain = row(norm_final) if layer == depth - 1 else None
        if layer % n_mixers == 0:
            q, k, v, lf, sg = _hgrn_pre(x2, row(norm_mix[layer]), hgrn_w_in[j].astype(BF16), lbp, j)
            to3 = lambda a: a.reshape(b, s, d)
            mix = _hgrn_scan(to3(q), to3(k), to3(v), to3(lf), to3(sg), row(hgrn_out_norm[j])).reshape(b * s, d)
            wo = hgrn_w_out[j]
        else:
            proj = _norm_matmul(x2, row(norm_mix[layer]), attn_w_qkv[j].astype(BF16), tn=3 * d)
            proj = proj.reshape(b, s, N_GROUPS * 3 * d)
            outs, lses = [], []
            for g, (window, dilation) in enumerate(DILATED_GROUPS):
                assert window // dilation == ATTN_BLOCK
                o_g, lse_g = _attn_group(proj, g, dilation)
                outs.append(o_g)
                lses.append(lse_g)
            mix = (*outs, *lses)
            wo = attn_w_out[j]
        x2 = _post(x2, mix, wo.astype(BF16), row(norm_mlp[layer]), mlp_w1[layer].astype(BF16),
                   mlp_w2[layer].astype(BF16), final_gain)
    return x2.reshape(b, s, d)
```

```python
import functools

import numpy as np
import jax
import jax.numpy as jnp
from jax import lax
from jax.experimental import pallas as pl
from jax.experimental.pallas import tpu as pltpu

F32 = jnp.float32
BF16 = jnp.bfloat16

D_MODEL = 1024
NORM_EPS = 1e-6
MLP_HIDDEN = 4 * D_MODEL

HGRN_HEAD_DIM = 128
HGRN_HEADS = D_MODEL // HGRN_HEAD_DIM
HGRN_CHUNK = 128

ATTN_HEADS = 16
ATTN_HEAD_DIM = D_MODEL // ATTN_HEADS
DILATED_GROUPS = ((128, 1), (512, 4), (2048, 16))
N_GROUPS = len(DILATED_GROUPS)
ATTN_BLOCK = 128
LANES = 128
HEAD_PAIRS = ATTN_HEADS // 2

VMEM_LIMIT_BYTES = 56 * 1024 * 1024

ROW_TILE = 512


def _dot(a, b):
    return jnp.dot(a, b, preferred_element_type=F32)


def _dot_nt(a, b):
    return lax.dot_general(a, b, (((1,), (1,)), ((), ())), preferred_element_type=F32)


def _dot_tn(a, b):
    return lax.dot_general(a, b, (((0,), (0,)), ((), ())), preferred_element_type=F32)


def _split_bf16(x):
    hi = x.astype(BF16)
    return hi, (x - hi.astype(F32)).astype(BF16)


def _rmsnorm_rows(x, gain):
    return x * lax.rsqrt(jnp.mean(x * x, axis=-1, keepdims=True) + NORM_EPS) * gain


def _params(*semantics):
    return pltpu.CompilerParams(dimension_semantics=semantics, vmem_limit_bytes=VMEM_LIMIT_BYTES)


def _const_spec(shape):
    zeros = (0,) * len(shape)
    return pl.BlockSpec(shape, lambda *_: zeros, pipeline_mode=pl.Buffered(1))


def _hgrn_pre_kernel(layer_j, x_ref, gain_ref, w_ref, lbp_ref, q_ref, k_ref, v_ref, lf_ref, sg_ref):
    d = D_MODEL
    h = _rmsnorm_rows(x_ref[...], gain_ref[...]).astype(BF16)

    p = lbp_ref[...]
    e = jnp.exp(p - jnp.max(p, axis=0, keepdims=True))
    sm = e / jnp.sum(e, axis=0, keepdims=True)
    lb = jnp.zeros((1, d), F32)
    for i in range(1, layer_j + 1):
        lb = lb + sm[i:i + 1, :]
    log_lb = jnp.log(lb)
    log_1m_lb = jnp.log1p(-lb)

    q = _dot(h, w_ref[:, 0:d])
    q_ref[...] = (q * (HGRN_HEAD_DIM ** -0.5) / (1.0 + jnp.exp(-q))).astype(BF16)

    f = _dot(h, w_ref[:, d:2 * d])
    ef = jnp.exp(-jnp.abs(f))
    inv = 1.0 / (1.0 + ef)
    log_sig = jnp.minimum(f, 0.0) - jnp.log1p(ef)
    b = log_1m_lb + log_sig
    lf_ref[...] = jnp.maximum(log_lb, b) + jnp.log1p(jnp.exp(-jnp.abs(log_lb - b)))
    sig_neg = jnp.where(f >= 0.0, ef * inv, inv)
    k_ref[...] = ((1.0 - lb) * sig_neg).astype(BF16)

    v_ref[...] = _dot(h, w_ref[:, 2 * d:3 * d]).astype(BF16)

    g = _dot(h, w_ref[:, 3 * d:4 * d])
    sg_ref[...] = (g / (1.0 + jnp.exp(-g))).astype(BF16)


def _hgrn_pre(x2, gain, w_in, lbp, layer_j):
    t, d = x2.shape
    tm = min(ROW_TILE, t)
    row = lambda i: (i, 0)
    out_bf = jax.ShapeDtypeStruct((t, d), BF16)
    return pl.pallas_call(
        functools.partial(_hgrn_pre_kernel, layer_j),
        grid=(t // tm,),
        in_specs=[pl.BlockSpec((tm, d), row), _const_spec((1, d)), _const_spec((d, 4 * d)),
                  _const_spec(lbp.shape)],
        out_specs=[pl.BlockSpec((tm, d), row)] * 5,
        out_shape=[out_bf, out_bf, out_bf, jax.ShapeDtypeStruct((t, d), F32), out_bf],
        compiler_params=_params("arbitrary"),
        name="hgrn_pre",
    )(x2, gain, w_in, lbp)


def _hgrn_levels(c):
    m, out = 1, []
    while m < c:
        out.append(m)
        m *= 2
    return out


def _hgrn_tables(c):
    levels = _hgrn_levels(c)
    idx = np.arange(c)
    x, j = idx[:, None], idx[None, :]
    blocks = [(j <= x), (j > x)]
    masks = [(x == j)]
    for m in levels:
        bnd = (x // (2 * m)) * (2 * m) + m - 1
        second = (x % (2 * m)) >= m
        blocks.append(np.where(second, (j > bnd) & (j <= x), (j > x) & (j <= bnd)))
        t, s = x, j
        masks.append((t // (2 * m) == s // (2 * m)) & ((t % (2 * m)) >= m) & ((s % (2 * m)) < m))
    a = np.concatenate(blocks, axis=0).astype(np.float32)
    a2 = np.concatenate([a, a], axis=1)
    return jnp.asarray(a2, BF16), jnp.asarray(np.stack(masks).astype(np.float32))


def _hgrn_scan_kernel(q_ref, k_ref, v_ref, lf_ref, sg_ref, a2_ref, mask_ref, onorm_ref, y_ref,
                      state_ref, g_ref):
    c = HGRN_CHUNK
    hd = HGRN_HEAD_DIM
    n_levels = len(_hgrn_levels(c))

    @pl.when(pl.program_id(1) == 0)
    def _():
        state_ref[...] = jnp.zeros_like(state_ref)

    lf_hi, lf_lo = _split_bf16(lf_ref[0])
    g_ref[...] = _dot(a2_ref[...], jnp.concatenate([lf_hi, lf_lo], axis=0))

    for h in range(HGRN_HEADS):
        sl = slice(h * hd, (h + 1) * hd)
        q_bf = q_ref[0, :, sl]
        k_bf = k_ref[0, :, sl]
        v_bf = v_ref[0, :, sl]
        q = q_bf.astype(F32)
        k = k_bf.astype(F32)
        cum = g_ref[0:c, sl]
        st = state_ref[h]

        o = _dot_nt((q * jnp.exp(cum)).astype(BF16), st.astype(BF16))
        scores = mask_ref[0] * _dot_nt(q_bf, k_bf)
        for li in range(n_levels):
            e = jnp.exp(g_ref[(2 + li) * c:(3 + li) * c, sl])
            scores = scores + mask_ref[1 + li] * _dot_nt((q * e).astype(BF16), (k * e).astype(BF16))
        o = o + _dot(scores.astype(BF16), v_bf)

        k_dec = (k * jnp.exp(g_ref[c:2 * c, sl])).astype(BF16)
        state_ref[h] = st * jnp.exp(cum[c - 1:c, :]) + _dot_tn(v_bf, k_dec)

        o = o * lax.rsqrt(jnp.mean(o * o, axis=-1, keepdims=True) + NORM_EPS)
        y_ref[0, :, sl] = (o * onorm_ref[:, sl] * sg_ref[0, :, sl].astype(F32)).astype(BF16)


def _hgrn_scan(q, k, v, lf, sg, out_norm):
    b, s, d = q.shape
    c = HGRN_CHUNK
    a2, masks = _hgrn_tables(c)
    tok = pl.BlockSpec((1, c, d), lambda i, j: (i, j, 0))
    return pl.pallas_call(
        _hgrn_scan_kernel,
        grid=(b, s // c),
        in_specs=[tok, tok, tok, tok, tok, _const_spec(a2.shape), _const_spec(masks.shape),
                  _const_spec((1, d))],
        out_specs=tok,
        out_shape=jax.ShapeDtypeStruct((b, s, d), BF16),
        scratch_shapes=[pltpu.VMEM((HGRN_HEADS, HGRN_HEAD_DIM, HGRN_HEAD_DIM), F32),
                        pltpu.VMEM((a2.shape[0], d), F32)],
        compiler_params=_params("arbitrary", "arbitrary"),
        name="hgrn_scan",
    )(q, k, v, lf, sg, a2, masks, out_norm)


def _tile_perm(dilation):
    n = ROW_TILE // dilation
    p = np.zeros((ROW_TILE, ROW_TILE), np.float32)
    for r in range(dilation):
        for m in range(n):
            p[r * n + m, m * dilation + r] = 1.0
    return p


def _span_tiles(dilation):
    return max(1, ATTN_BLOCK * dilation // ROW_TILE)


def _grouped_view(a, dilation):
    t, n = a.shape
    k = _span_tiles(dilation)
    if k == 1:
        return a
    return a.reshape(t // (k * ROW_TILE), dilation, k, ROW_TILE // dilation, n)


def _grouped_spec(n, dilation):
    k = _span_tiles(dilation)
    if k == 1:
        return pl.BlockSpec((ROW_TILE, n), lambda i, *_: (i, 0))
    return pl.BlockSpec((None, dilation, None, ROW_TILE // dilation, n), lambda i, *_: (i // k, 0, i % k, 0, 0))


def _grouped_store(ref, dilation, val):
    if _span_tiles(dilation) == 1:
        ref[...] = val
    else:
        n = ROW_TILE // dilation
        for r in range(dilation):
            ref[r] = val[r * n:(r + 1) * n]


def _grouped_load(ref, dilation):
    if _span_tiles(dilation) == 1:
        return ref[...]
    return jnp.concatenate([ref[r] for r in range(dilation)], axis=0)


def _attn_qkv_kernel(x_ref, gain_ref, p1_ref, p2_ref, w_ref, o0_ref, o1_ref, o2_ref, h_ref):
    d = D_MODEL
    g = pl.program_id(1)

    @pl.when(g == 0)
    def _():
        h = _rmsnorm_rows(x_ref[...], gain_ref[...]).astype(BF16)
        h_ref[0] = h
        h_ref[1] = _dot(p1_ref[...], h).astype(BF16)
        h_ref[2] = _dot(p2_ref[...], h).astype(BF16)

    for gi, o_ref in enumerate((o0_ref, o1_ref, o2_ref)):
        @pl.when(g == gi)
        def _(gi=gi, o_ref=o_ref):
            res = _dot(h_ref[gi], w_ref[...])
            val = jnp.concatenate([(res[:, :d] * (ATTN_HEAD_DIM ** -0.5)).astype(BF16),
                                   res[:, d:].astype(BF16)], axis=1)
            _grouped_store(o_ref, DILATED_GROUPS[gi][1], val)


def _attn_qkv(x2, gain, w):
    t, d = x2.shape
    n = 3 * d
    dils = [dil for _, dil in DILATED_GROUPS]
    assert dils[0] == 1 and N_GROUPS == 3
    perms = [jnp.asarray(_tile_perm(dil), BF16) for dil in dils[1:]]
    outs = pl.pallas_call(
        _attn_qkv_kernel,
        grid=(t // ROW_TILE, N_GROUPS),
        in_specs=[pl.BlockSpec((ROW_TILE, d), lambda i, g: (i, 0)), _const_spec((1, d)),
                  _const_spec(perms[0].shape), _const_spec(perms[1].shape),
                  pl.BlockSpec((d, n), lambda i, g: (0, g))],
        out_specs=[_grouped_spec(n, dil) for dil in dils],
        out_shape=[jax.ShapeDtypeStruct(_grouped_view(jnp.empty((t, n), BF16), dil).shape, BF16) for dil in dils],
        scratch_shapes=[pltpu.VMEM((N_GROUPS, ROW_TILE, d), BF16)],
        compiler_params=_params("arbitrary", "arbitrary"),
        name="attn_qkv",
    )(x2, gain, *perms, w)
    return [o.reshape(t, n) for o in outs]


def _attn_group_kernel(dilation, blocks_per_seq, q_ref, kp_ref, kc_ref, vp_ref, vc_ref, lm_ref, o_ref, lse_ref):
    blk = ATTN_BLOCK
    m_lo, m_hi = lm_ref[0], lm_ref[1]
    has_prev = (pl.program_id(0) % blocks_per_seq) >= dilation
    row = lax.broadcasted_iota(jnp.int32, (2 * blk, 2 * blk), 0) % blk
    col = lax.broadcasted_iota(jnp.int32, (2 * blk, 2 * blk), 1)
    neg_inf = F32(-jnp.inf)
    bias = jnp.where(col < blk,
                     jnp.where((col >= row) & has_prev, F32(0.0), neg_inf),
                     jnp.where((col - blk) <= row, F32(0.0), neg_inf))
    lane = lax.broadcasted_iota(jnp.int32, (blk, LANES), 1)
    low_q = lane < ATTN_HEAD_DIM
    lse_tile = jnp.zeros((blk, LANES), F32)

    for p in range(HEAD_PAIRS):
        sl = slice(p * LANES, (p + 1) * LANES)
        qp = q_ref[:, sl]
        q_bd = jnp.concatenate([qp * m_lo[:blk], qp * m_hi[:blk]], axis=0)
        k_cat = jnp.concatenate([kp_ref[:, sl], kc_ref[:, sl]], axis=0)
        s = _dot_nt(q_bd, k_cat) + bias
        s0, s1 = s[:blk], s[blk:]
        m0 = jnp.max(s0, axis=-1, keepdims=True)
        m1 = jnp.max(s1, axis=-1, keepdims=True)
        p_cat = jnp.concatenate([jnp.exp(s0 - m0).astype(BF16), jnp.exp(s1 - m1).astype(BF16)], axis=1)
        v_cat = jnp.concatenate([vp_ref[:, sl], vc_ref[:, sl]], axis=0)
        rhs = jnp.concatenate([jnp.concatenate([v_cat * m_lo, m_lo], axis=1),
                               jnp.concatenate([v_cat * m_hi, m_hi], axis=1)], axis=0)
        res = _dot(p_cat, rhs)
        denom = res[:, LANES:]
        o_ref[:, sl] = (res[:, :LANES] / denom).astype(o_ref.dtype)
        m_pair = jnp.where(low_q, m0, m1)
        lse_tile = jnp.where((lane == p) | (lane == ATTN_HEAD_DIM + p), m_pair + jnp.log(denom), lse_tile)

    lse_ref[...] = lse_tile


def _attn_group(qkv, dilation, seq):
    t, n = qkv.shape
    d = D_MODEL
    blk = ATTN_BLOCK

    def spec(which, prev):
        if prev:
            return pl.BlockSpec((blk, d), lambda i: (jnp.maximum(i - dilation, 0), which))
        return pl.BlockSpec((blk, d), lambda i: (i, which))

    first = np.broadcast_to(np.arange(LANES) < ATTN_HEAD_DIM, (2 * blk, LANES))
    lane_masks = jnp.asarray(np.stack([first, ~first]).astype(np.float32), BF16)

    return pl.pallas_call(
        functools.partial(_attn_group_kernel, dilation, seq // blk),
        grid=(t // blk,),
        in_specs=[spec(0, False), spec(1, True), spec(1, False), spec(2, True), spec(2, False),
                  _const_spec(lane_masks.shape)],
        out_specs=[pl.BlockSpec((blk, d), lambda i: (i, 0)), pl.BlockSpec((blk, LANES), lambda i: (i, 0))],
        out_shape=[jax.ShapeDtypeStruct((t, d), BF16), jax.ShapeDtypeStruct((t, LANES), F32)],
        compiler_params=_params("arbitrary"),
        name=f"attn_group_d{dilation}",
    )(qkv, qkv, qkv, qkv, qkv, lane_masks)


def _head_expand_matrix():
    e = np.zeros((LANES, D_MODEL), np.float32)
    for h in range(ATTN_HEADS):
        lane = (h // 2) + ATTN_HEAD_DIM * (h % 2)
        e[lane, h * ATTN_HEAD_DIM:(h + 1) * ATTN_HEAD_DIM] = 1.0
    return jnp.asarray(e, BF16)


def _mlp_tail(x1, gain_ref, w1_ref, w2_ref, final_gain_ref, o_ref):
    d = D_MODEL
    h = _rmsnorm_rows(x1, gain_ref[...]).astype(BF16)
    acc = x1
    for c in range(MLP_HIDDEN // d):
        a = jnp.maximum(_dot(h, w1_ref[:, c * d:(c + 1) * d]), 0.0)
        acc = acc + _dot((a * a).astype(BF16), w2_ref[c * d:(c + 1) * d, :])
    if final_gain_ref is not None:
        acc = _rmsnorm_rows(acc, final_gain_ref[...])
    o_ref[...] = acc


def _post_kernel(final, x_ref, y_ref, wo_ref, gain_ref, w1_ref, w2_ref, *rest):
    if final:
        final_gain_ref, o_ref = rest
    else:
        final_gain_ref, (o_ref,) = None, rest
    x1 = x_ref[...] + _dot(y_ref[...], wo_ref[...])
    _mlp_tail(x1, gain_ref, w1_ref, w2_ref, final_gain_ref, o_ref)


def _attn_post_kernel(final, x_ref, o0_ref, o1_ref, o2_ref, l0_ref, l1_ref, l2_ref, u1_ref, u2_ref, exp_ref,
                      wo_ref, gain_ref, w1_ref, w2_ref, *rest):
    if final:
        final_gain_ref, o_ref = rest
    else:
        final_gain_ref, (o_ref,) = None, rest
    dils = [dil for _, dil in DILATED_GROUPS]
    outs = [o0_ref[...]]
    lses = [l0_ref[...]]
    for o_g, l_g, u_ref, dil in zip((o1_ref, o2_ref), (l1_ref, l2_ref), (u1_ref, u2_ref), dils[1:]):
        outs.append(_dot(u_ref[...], _grouped_load(o_g, dil)).astype(BF16))
        l_hi, l_lo = _split_bf16(_grouped_load(l_g, dil))
        lses.append(_dot(u_ref[...], l_hi) + _dot(u_ref[...], l_lo))
    m = jnp.maximum(jnp.maximum(lses[0], lses[1]), lses[2])
    es = [jnp.exp(l - m) for l in lses]
    inv = 1.0 / (es[0] + es[1] + es[2])
    y = None
    for e, o_nat in zip(es, outs):
        w_hi, w_lo = _split_bf16(e * inv)
        w_full = _dot(w_hi, exp_ref[...]) + _dot(w_lo, exp_ref[...])
        term = w_full * o_nat.astype(F32)
        y = term if y is None else y + term
    x1 = x_ref[...] + _dot(y.astype(BF16), wo_ref[...])
    _mlp_tail(x1, gain_ref, w1_ref, w2_ref, final_gain_ref, o_ref)


def _post(x2, mix, wo, gain, w1, w2, final_gain):
    t, d = x2.shape
    assert t % ROW_TILE == 0
    row = lambda i: (i, 0)
    final = final_gain is not None
    weights = [wo, gain, w1, w2] + ([final_gain] if final else [])
    weight_specs = [_const_spec(w.shape) for w in weights]
    if isinstance(mix, tuple):
        dils = [dil for _, dil in DILATED_GROUPS]
        unperms = [jnp.asarray(_tile_perm(dil).T, BF16) for dil in dils[1:]]
        expand = _head_expand_matrix()
        body = functools.partial(_attn_post_kernel, final)
        views = [_grouped_view(a, dil) for a, dil in zip(mix, dils + dils)]
        args = [x2, *views, *unperms, expand] + weights
        in_specs = ([pl.BlockSpec((ROW_TILE, d), row)] + [_grouped_spec(d, dil) for dil in dils]
                    + [_grouped_spec(LANES, dil) for dil in dils]
                    + [_const_spec(u.shape) for u in unperms] + [_const_spec(expand.shape)] + weight_specs)
    else:
        body = functools.partial(_post_kernel, final)
        args = [x2, mix] + weights
        in_specs = [pl.BlockSpec((ROW_TILE, d), row)] * 2 + weight_specs
    return pl.pallas_call(
        body,
        grid=(t // ROW_TILE,),
        in_specs=in_specs,
        out_specs=pl.BlockSpec((ROW_TILE, d), row),
        out_shape=jax.ShapeDtypeStruct((t, d), F32),
        compiler_params=_params("arbitrary"),
        name="post_mlp",
    )(*args)


def kernel(x, norm_mix, norm_mlp, norm_final, hgrn_w_in, hgrn_lower_bound, hgrn_out_norm, hgrn_w_out,
           attn_w_qkv, attn_w_out, mlp_w1, mlp_w2):
    b, s, d = x.shape
    depth = norm_mix.shape[0]
    n_mixers = 2
    x2 = x.reshape(b * s, d)
    row = lambda v: v.reshape(1, -1).astype(F32)
    lbp = hgrn_lower_bound.astype(F32)

    for layer in range(depth):
        j = layer // n_mixers
        final_gain = row(norm_final) if layer == depth - 1 else None
        if layer % n_mixers == 0:
            q, k, v, lf, sg = _hgrn_pre(x2, row(norm_mix[layer]), hgrn_w_in[j].astype(BF16), lbp, j)
            to3 = lambda a: a.reshape(b, s, d)
            mix = _hgrn_scan(to3(q), to3(k), to3(v), to3(lf), to3(sg), row(hgrn_out_norm[j])).reshape(b * s, d)
            wo = hgrn_w_out[j]
        else:
            qkvs = _attn_qkv(x2, row(norm_mix[layer]), attn_w_qkv[j].astype(BF16))
            outs, lses = [], []
            for qkv, (window, dilation) in zip(qkvs, DILATED_GROUPS):
                assert window // dilation == ATTN_BLOCK and s % (ATTN_BLOCK * dilation) == 0
                o_g, lse_g = _attn_group(qkv, dilation, s)
                outs.append(o_g)
                lses.append(lse_g)
            mix = (*outs, *lses)
            wo = attn_w_out[j]
        x2 = _post(x2, mix, wo.astype(BF16), row(norm_mlp[layer]), mlp_w1[layer].astype(BF16),
                   mlp_w2[layer].astype(BF16), final_gain)
    return x2.reshape(b, s, d)
```

```python
import functools

import numpy as np
import jax
import jax.numpy as jnp
from jax import lax
from jax.experimental import pallas as pl
from jax.experimental.pallas import tpu as pltpu

F32 = jnp.float32
BF16 = jnp.bfloat16

LOG2_E = 1.4426950408889634
LN_2 = 0.6931471805599453

D_MODEL = 1024
NORM_EPS = 1e-6
MLP_HIDDEN = 4 * D_MODEL

HGRN_HEAD_DIM = 128
HGRN_HEADS = D_MODEL // HGRN_HEAD_DIM
HGRN_CHUNK = 128
HGRN_COMPACT_FROM = 16

ATTN_HEADS = 16
ATTN_HEAD_DIM = D_MODEL // ATTN_HEADS
DILATED_GROUPS = ((128, 1), (512, 4), (2048, 16))
N_GROUPS = len(DILATED_GROUPS)
ATTN_BLOCK = 128
ATTN_STEP_BLOCKS = 2
LANES = 128
HEAD_PAIRS = ATTN_HEADS // 2

VMEM_LIMIT_BYTES = 56 * 1024 * 1024

ROW_TILE = 512


def _dot(a, b):
    return jnp.dot(a, b, preferred_element_type=F32)


def _dot_nt(a, b):
    return lax.dot_general(a, b, (((1,), (1,)), ((), ())), preferred_element_type=F32)


def _dot_tn(a, b):
    return lax.dot_general(a, b, (((0,), (0,)), ((), ())), preferred_element_type=F32)


def _split_bf16(x):
    hi = x.astype(BF16)
    return hi, (x - hi.astype(F32)).astype(BF16)


def _rmsnorm_rows(x, gain):
    return x * lax.rsqrt(jnp.mean(x * x, axis=-1, keepdims=True) + NORM_EPS) * gain


def _params(*semantics):
    return pltpu.CompilerParams(dimension_semantics=semantics, vmem_limit_bytes=VMEM_LIMIT_BYTES)


def _const_spec(shape):
    zeros = (0,) * len(shape)
    return pl.BlockSpec(shape, lambda *_: zeros, pipeline_mode=pl.Buffered(1))


def _hgrn_pre_kernel(layer_j, x_ref, gain_ref, w_ref, lbp_ref, q_ref, k_ref, v_ref, lf_ref, sg_ref):
    d = D_MODEL
    h = _rmsnorm_rows(x_ref[...], gain_ref[...]).astype(BF16)

    q = _dot(h, w_ref[:, 0:d])
    q_ref[...] = (q * (HGRN_HEAD_DIM ** -0.5) / (1.0 + jnp.exp(-q))).astype(BF16)

    f = _dot(h, w_ref[:, d:2 * d])
    f2 = f * LOG2_E
    ef = jnp.exp2(-jnp.abs(f2))
    inv = 1.0 / (1.0 + ef)
    log_sig = jnp.minimum(f2, 0.0) - jnp.log2(1.0 + ef)
    sig_neg = jnp.where(f >= 0.0, ef * inv, inv)
    if layer_j == 0:
        lf_ref[...] = log_sig
        k_ref[...] = sig_neg.astype(BF16)
    else:
        p = lbp_ref[...]
        e = jnp.exp(p - jnp.max(p, axis=0, keepdims=True))
        sm = e / jnp.sum(e, axis=0, keepdims=True)
        lb = sm[1:2, :]
        for i in range(2, layer_j + 1):
            lb = lb + sm[i:i + 1, :]
        a = jnp.log2(lb)
        b = jnp.log2(1.0 - lb) + log_sig
        lf_ref[...] = jnp.maximum(a, b) + jnp.log2(1.0 + jnp.exp2(-jnp.abs(a - b)))
        k_ref[...] = ((1.0 - lb) * sig_neg).astype(BF16)

    v_ref[...] = _dot(h, w_ref[:, 2 * d:3 * d]).astype(BF16)

    g = _dot(h, w_ref[:, 3 * d:4 * d])
    sg_ref[...] = (g / (1.0 + jnp.exp(-g))).astype(BF16)


def _hgrn_pre(x2, gain, w_in, lbp, layer_j):
    t, d = x2.shape
    tm = min(ROW_TILE, t)
    row = lambda i: (i, 0)
    out_bf = jax.ShapeDtypeStruct((t, d), BF16)
    return pl.pallas_call(
        functools.partial(_hgrn_pre_kernel, layer_j),
        grid=(t // tm,),
        in_specs=[pl.BlockSpec((tm, d), row), _const_spec((1, d)), _const_spec((d, 4 * d)),
                  _const_spec(lbp.shape)],
        out_specs=[pl.BlockSpec((tm, d), row)] * 5,
        out_shape=[out_bf, out_bf, out_bf, jax.ShapeDtypeStruct((t, d), F32), out_bf],
        compiler_params=_params("arbitrary"),
        name="hgrn_pre",
    )(x2, gain, w_in, lbp)


def _hgrn_levels(c):
    m, out = 1, []
    while m < c:
        out.append(m)
        m *= 2
    return out


def _hgrn_tables(c):
    levels = _hgrn_levels(c)
    idx = np.arange(c)
    x, j = idx[:, None], idx[None, :]
    blocks = [(j <= x), (j > x)]
    masks = [(x == j)]
    for m in levels:
        bnd = (x // (2 * m)) * (2 * m) + m - 1
        second = (x % (2 * m)) >= m
        blocks.append(np.where(second, (j > bnd) & (j <= x), (j > x) & (j <= bnd)))
        t, s = x, j
        mask = (t // (2 * m) == s // (2 * m)) & ((t % (2 * m)) >= m) & ((s % (2 * m)) < m)
        if m < HGRN_COMPACT_FROM:
            masks.append(mask)
        elif c // (2 * m) > 1:
            masks.append(mask[second[:, 0]])
    a = np.concatenate(blocks, axis=0).astype(np.float32)
    a2 = np.concatenate([a, a], axis=1)
    masks = np.concatenate(masks, axis=0).astype(np.float32)
    masks2 = np.concatenate([masks, masks], axis=1)
    return jnp.asarray(a2, BF16), jnp.asarray(masks2)


def _hgrn_scan_kernel(q_ref, k_ref, v_ref, lf_ref, sg_ref, a2_ref, mask_ref, onorm_ref, y_ref,
                      state_ref, g_ref):
    c = HGRN_CHUNK
    hd = HGRN_HEAD_DIM
    levels = _hgrn_levels(c)

    @pl.when(pl.program_id(1) == 0)
    def _():
        state_ref[...] = jnp.zeros_like(state_ref)

    lf_hi, lf_lo = _split_bf16(lf_ref[0])
    g_ref[...] = _dot(a2_ref[...], jnp.concatenate([lf_hi, lf_lo], axis=0))

    zeros = jnp.zeros((c, hd), BF16)
    zeros2 = jnp.zeros((c, 2 * hd), BF16)

    def block_diag(x):
        return jnp.concatenate([jnp.concatenate([x[:, :hd], zeros[:x.shape[0]]], axis=1),
                                jnp.concatenate([zeros[:x.shape[0]], x[:, hd:]], axis=1)], axis=0)

    def decayed(x_bf, block, rows=slice(0, c)):
        e = jnp.exp2(g_ref[block * c + rows.start:block * c + rows.stop, sl])
        return x_bf[rows] * e.astype(BF16)

    tile = 16
    for p in range(HGRN_HEADS // 2):
        sl = slice(2 * p * hd, 2 * (p + 1) * hd)
        q_bf = q_ref[0, :, sl]
        k_bf = k_ref[0, :, sl]
        v_bf = v_ref[0, :, sl]
        st = [state_ref[2 * p], state_ref[2 * p + 1]]
        st_bd = block_diag(jnp.concatenate([st[0].astype(BF16), st[1].astype(BF16)], axis=1))

        o = _dot_nt(decayed(q_bf, 0), st_bd)

        pieces = [None] * (c // tile)

        def add_rows(start, val):
            for i in range(val.shape[0] // tile):
                piece = val[i * tile:(i + 1) * tile]
                at = start // tile + i
                pieces[at] = piece if pieces[at] is None else pieces[at] + piece

        add_rows(0, mask_ref[0:c] * _dot_nt(q_bf, block_diag(k_bf)))
        mrow = c
        for li, m in enumerate(levels):
            blk = 2 + li
            if m < HGRN_COMPACT_FROM:
                s = _dot_nt(decayed(q_bf, blk), block_diag(decayed(k_bf, blk)))
                add_rows(0, mask_ref[mrow:mrow + c] * s)
                mrow += c
            else:
                n_blk = c // (2 * m)
                first = [slice(b * 2 * m, b * 2 * m + m) for b in range(n_blk)]
                second = [slice(b * 2 * m + m, (b + 1) * 2 * m) for b in range(n_blk)]
                lhs = jnp.concatenate([decayed(q_bf, blk, r) for r in second], axis=0)
                keys = jnp.concatenate([x for r in first for x in (decayed(k_bf, blk, r), zeros2[:m])], axis=0)
                s = _dot_nt(lhs, block_diag(keys))
                if n_blk > 1:
                    s = mask_ref[mrow:mrow + c // 2] * s
                    mrow += c // 2
                for b, r in enumerate(second):
                    add_rows(r.start, s[b * m:(b + 1) * m])
        scores = jnp.concatenate(pieces, axis=0)
        o = o + _dot(scores.astype(BF16), block_diag(v_bf))

        d_state = _dot_tn(v_bf, decayed(k_bf, 1))
        decay_all = jnp.exp2(g_ref[c - 1:c, sl])
        for i in range(2):
            hs = slice(i * hd, (i + 1) * hd)
            state_ref[2 * p + i] = st[i] * decay_all[:, hs] + d_state[hs, hs]
            oh = o[:, hs]
            oh = oh * lax.rsqrt(jnp.mean(oh * oh, axis=-1, keepdims=True) + NORM_EPS)
            lanes = slice((2 * p + i) * hd, (2 * p + i + 1) * hd)
            y_ref[0, :, lanes] = (oh * onorm_ref[:, lanes] * sg_ref[0, :, lanes].astype(F32)).astype(BF16)


def _hgrn_scan(q, k, v, lf, sg, out_norm):
    b, s, d = q.shape
    c = HGRN_CHUNK
    a2, masks = _hgrn_tables(c)
    tok = pl.BlockSpec((1, c, d), lambda i, j: (i, j, 0))
    return pl.pallas_call(
        _hgrn_scan_kernel,
        grid=(b, s // c),
        in_specs=[tok, tok, tok, tok, tok, _const_spec(a2.shape), _const_spec(masks.shape),
                  _const_spec((1, d))],
        out_specs=tok,
        out_shape=jax.ShapeDtypeStruct((b, s, d), BF16),
        scratch_shapes=[pltpu.VMEM((HGRN_HEADS, HGRN_HEAD_DIM, HGRN_HEAD_DIM), F32),
                        pltpu.VMEM((a2.shape[0], d), F32)],
        compiler_params=_params("arbitrary", "arbitrary"),
        name="hgrn_scan",
    )(q, k, v, lf, sg, a2, masks, out_norm)


def _tile_perm(dilation):
    n = ROW_TILE // dilation
    p = np.zeros((ROW_TILE, ROW_TILE), np.float32)
    for r in range(dilation):
        for m in range(n):
            p[r * n + m, m * dilation + r] = 1.0
    return p


def _span_tiles(dilation):
    return max(1, ATTN_BLOCK * dilation // ROW_TILE)


def _grouped_view(a, dilation):
    t, n = a.shape
    k = _span_tiles(dilation)
    if k == 1:
        return a
    return a.reshape(t // (k * ROW_TILE), dilation, k, ROW_TILE // dilation, n)


def _grouped_spec(n, dilation):
    k = _span_tiles(dilation)
    if k == 1:
        return pl.BlockSpec((ROW_TILE, n), lambda i, *_: (i, 0))
    return pl.BlockSpec((None, dilation, None, ROW_TILE // dilation, n), lambda i, *_: (i // k, 0, i % k, 0, 0))


def _grouped_store(ref, dilation, val):
    if _span_tiles(dilation) == 1:
        ref[...] = val
    else:
        n = ROW_TILE // dilation
        for r in range(dilation):
            ref[r] = val[r * n:(r + 1) * n]


def _grouped_load(ref, dilation):
    if _span_tiles(dilation) == 1:
        return ref[...]
    return jnp.concatenate([ref[r] for r in range(dilation)], axis=0)


def _attn_qkv_kernel(x_ref, gain_ref, p1_ref, p2_ref, w_ref, o0_ref, o1_ref, o2_ref, h_ref):
    d = D_MODEL
    g = pl.program_id(1)

    @pl.when(g == 0)
    def _():
        h = _rmsnorm_rows(x_ref[...], gain_ref[...]).astype(BF16)
        h_ref[0] = h
        h_ref[1] = _dot(p1_ref[...], h).astype(BF16)
        h_ref[2] = _dot(p2_ref[...], h).astype(BF16)

    for gi, o_ref in enumerate((o0_ref, o1_ref, o2_ref)):
        @pl.when(g == gi)
        def _(gi=gi, o_ref=o_ref):
            res = _dot(h_ref[gi], w_ref[...])
            val = jnp.concatenate([(res[:, :d] * (ATTN_HEAD_DIM ** -0.5 * LOG2_E)).astype(BF16),
                                   res[:, d:].astype(BF16)], axis=1)
            _grouped_store(o_ref, DILATED_GROUPS[gi][1], val)


def _attn_qkv(x2, gain, w):
    t, d = x2.shape
    n = 3 * d
    dils = [dil for _, dil in DILATED_GROUPS]
    assert dils[0] == 1 and N_GROUPS == 3
    perms = [jnp.asarray(_tile_perm(dil), BF16) for dil in dils[1:]]
    outs = pl.pallas_call(
        _attn_qkv_kernel,
        grid=(t // ROW_TILE, N_GROUPS),
        in_specs=[pl.BlockSpec((ROW_TILE, d), lambda i, g: (i, 0)), _const_spec((1, d)),
                  _const_spec(perms[0].shape), _const_spec(perms[1].shape),
                  pl.BlockSpec((d, n), lambda i, g: (0, g))],
        out_specs=[_grouped_spec(n, dil) for dil in dils],
        out_shape=[jax.ShapeDtypeStruct(_grouped_view(jnp.empty((t, n), BF16), dil).shape, BF16) for dil in dils],
        scratch_shapes=[pltpu.VMEM((N_GROUPS, ROW_TILE, d), BF16)],
        compiler_params=_params("arbitrary", "arbitrary"),
        name="attn_qkv",
    )(x2, gain, *perms, w)
    return [o.reshape(t, n) for o in outs]


def _attn_group_kernel(dilation, blocks_per_seq, q_ref, kp_ref, kc_ref, vp_ref, vc_ref, lm_ref, o_ref, lse_ref):
    blk = ATTN_BLOCK
    nb = ATTN_STEP_BLOCKS
    m_lo, m_hi = lm_ref[0], lm_ref[1]
    row = lax.broadcasted_iota(jnp.int32, (2 * blk, 2 * blk), 0) % blk
    col = lax.broadcasted_iota(jnp.int32, (2 * blk, 2 * blk), 1)
    neg_inf = F32(-jnp.inf)
    lane = lax.broadcasted_iota(jnp.int32, (blk, LANES), 1)
    low_q = lane < ATTN_HEAD_DIM

    for j in range(nb):
        rows = slice(j * blk, (j + 1) * blk)
        if dilation >= nb:
            prev_k, prev_v, prev_rows = kp_ref, vp_ref, rows
        elif j == 0:
            prev_k, prev_v, prev_rows = kp_ref, vp_ref, slice((nb - 1) * blk, nb * blk)
        else:
            prev_k, prev_v, prev_rows = kc_ref, vc_ref, slice((j - 1) * blk, j * blk)
        has_prev = ((pl.program_id(0) * nb + j) % blocks_per_seq) >= dilation
        bias = jnp.where(col < blk,
                         jnp.where((col >= row) & has_prev, F32(0.0), neg_inf),
                         jnp.where((col - blk) <= row, F32(0.0), neg_inf))
        lse_tile = jnp.zeros((blk, LANES), F32)

        for p in range(HEAD_PAIRS):
            sl = slice(p * LANES, (p + 1) * LANES)
            qp = q_ref[rows, sl]
            q_bd = jnp.concatenate([qp * m_lo[:blk], qp * m_hi[:blk]], axis=0)
            k_cat = jnp.concatenate([prev_k[prev_rows, sl], kc_ref[rows, sl]], axis=0)
            s = _dot_nt(q_bd, k_cat) + bias
            s0, s1 = s[:blk], s[blk:]
            m0 = jnp.max(s0, axis=-1, keepdims=True)
            m1 = jnp.max(s1, axis=-1, keepdims=True)
            p_cat = jnp.concatenate([jnp.exp2(s0 - m0).astype(BF16), jnp.exp2(s1 - m1).astype(BF16)], axis=1)
            v_cat = jnp.concatenate([prev_v[prev_rows, sl], vc_ref[rows, sl]], axis=0)
            rhs = jnp.concatenate([jnp.concatenate([v_cat * m_lo, m_lo], axis=1),
                                   jnp.concatenate([v_cat * m_hi, m_hi], axis=1)], axis=0)
            res = _dot(p_cat, rhs)
            denom = res[:, LANES:]
            o_ref[rows, sl] = (res[:, :LANES] / denom).astype(o_ref.dtype)
            m_pair = jnp.where(low_q, m0, m1)
            lse_tile = jnp.where((lane == p) | (lane == ATTN_HEAD_DIM + p), m_pair + jnp.log2(denom), lse_tile)

        lse_ref[rows, :] = lse_tile * LN_2


def _attn_group(qkv, dilation, seq):
    t, n = qkv.shape
    d = D_MODEL
    nb = ATTN_STEP_BLOCKS
    blk = nb * ATTN_BLOCK
    assert dilation == 1 or dilation % nb == 0
    back = max(dilation // nb, 1)

    def spec(which, prev):
        if prev:
            return pl.BlockSpec((blk, d), lambda i: (jnp.maximum(i - back, 0), which))
        return pl.BlockSpec((blk, d), lambda i: (i, which))

    first = np.broadcast_to(np.arange(LANES) < ATTN_HEAD_DIM, (2 * ATTN_BLOCK, LANES))
    lane_masks = jnp.asarray(np.stack([first, ~first]).astype(np.float32), BF16)

    return pl.pallas_call(
        functools.partial(_attn_group_kernel, dilation, seq // ATTN_BLOCK),
        grid=(t // blk,),
        in_specs=[spec(0, False), spec(1, True), spec(1, False), spec(2, True), spec(2, False),
                  _const_spec(lane_masks.shape)],
        out_specs=[pl.BlockSpec((blk, d), lambda i: (i, 0)), pl.BlockSpec((blk, LANES), lambda i: (i, 0))],
        out_shape=[jax.ShapeDtypeStruct((t, d), BF16), jax.ShapeDtypeStruct((t, LANES), F32)],
        compiler_params=_params("arbitrary"),
        name=f"attn_group_d{dilation}",
    )(qkv, qkv, qkv, qkv, qkv, lane_masks)


def _head_expand_matrix():
    e = np.zeros((LANES, D_MODEL), np.float32)
    for h in range(ATTN_HEADS):
        lane = (h // 2) + ATTN_HEAD_DIM * (h % 2)
        e[lane, h * ATTN_HEAD_DIM:(h + 1) * ATTN_HEAD_DIM] = 1.0
    return jnp.asarray(e, BF16)


def _mlp_tail(x1, gain_ref, w1_ref, w2_ref, final_gain_ref, o_ref):
    d = D_MODEL
    h = _rmsnorm_rows(x1, gain_ref[...]).astype(BF16)
    acc = x1
    for c in range(MLP_HIDDEN // d):
        a = jnp.maximum(_dot(h, w1_ref[:, c * d:(c + 1) * d]), 0.0)
        acc = acc + _dot((a * a).astype(BF16), w2_ref[c * d:(c + 1) * d, :])
    if final_gain_ref is not None:
        acc = _rmsnorm_rows(acc, final_gain_ref[...])
    o_ref[...] = acc


def _post_kernel(final, x_ref, y_ref, wo_ref, gain_ref, w1_ref, w2_ref, *rest):
    if final:
        final_gain_ref, o_ref = rest
    else:
        final_gain_ref, (o_ref,) = None, rest
    x1 = x_ref[...] + _dot(y_ref[...], wo_ref[...])
    _mlp_tail(x1, gain_ref, w1_ref, w2_ref, final_gain_ref, o_ref)


def _attn_post_kernel(final, x_ref, o0_ref, o1_ref, o2_ref, l0_ref, l1_ref, l2_ref, u1_ref, u2_ref, exp_ref,
                      wo_ref, gain_ref, w1_ref, w2_ref, *rest):
    if final:
        final_gain_ref, o_ref = rest
    else:
        final_gain_ref, (o_ref,) = None, rest
    dils = [dil for _, dil in DILATED_GROUPS]
    outs = [o0_ref[...]]
    lses = [l0_ref[...]]
    for o_g, l_g, u_ref, dil in zip((o1_ref, o2_ref), (l1_ref, l2_ref), (u1_ref, u2_ref), dils[1:]):
        outs.append(_dot(u_ref[...], _grouped_load(o_g, dil)).astype(BF16))
        both = _dot(u_ref[...], jnp.concatenate(_split_bf16(_grouped_load(l_g, dil)), axis=1))
        lses.append(both[:, :LANES] + both[:, LANES:])
    m = jnp.maximum(jnp.maximum(lses[0], lses[1]), lses[2])
    es = [jnp.exp(l - m) for l in lses]
    inv = 1.0 / (es[0] + es[1] + es[2])
    y = None
    for e, o_nat in zip(es, outs):
        term = _dot((e * inv).astype(BF16), exp_ref[...]) * o_nat.astype(F32)
        y = term if y is None else y + term
    x1 = x_ref[...] + _dot(y.astype(BF16), wo_ref[...])
    _mlp_tail(x1, gain_ref, w1_ref, w2_ref, final_gain_ref, o_ref)


def _post(x2, mix, wo, gain, w1, w2, final_gain):
    t, d = x2.shape
    assert t % ROW_TILE == 0
    row = lambda i: (i, 0)
    final = final_gain is not None
    weights = [wo, gain, w1, w2] + ([final_gain] if final else [])
    weight_specs = [_const_spec(w.shape) for w in weights]
    if isinstance(mix, tuple):
        dils = [dil for _, dil in DILATED_GROUPS]
        unperms = [jnp.asarray(_tile_perm(dil).T, BF16) for dil in dils[1:]]
        expand = _head_expand_matrix()
        body = functools.partial(_attn_post_kernel, final)
        views = [_grouped_view(a, dil) for a, dil in zip(mix, dils + dils)]
        args = [x2, *views, *unperms, expand] + weights
        in_specs = ([pl.BlockSpec((ROW_TILE, d), row)] + [_grouped_spec(d, dil) for dil in dils]
                    + [_grouped_spec(LANES, dil) for dil in dils]
                    + [_const_spec(u.shape) for u in unperms] + [_const_spec(expand.shape)] + weight_specs)
    else:
        body = functools.partial(_post_kernel, final)
        args = [x2, mix] + weights
        in_specs = [pl.BlockSpec((ROW_TILE, d), row)] * 2 + weight_specs
    return pl.pallas_call(
        body,
        grid=(t // ROW_TILE,),
        in_specs=in_specs,
        out_specs=pl.BlockSpec((ROW_TILE, d), row),
        out_shape=jax.ShapeDtypeStruct((t, d), F32),
        compiler_params=_params("arbitrary"),
        name="post_mlp",
    )(*args)


def kernel(x, norm_mix, norm_mlp, norm_final, hgrn_w_in, hgrn_lower_bound, hgrn_out_norm, hgrn_w_out,
           attn_w_qkv, attn_w_out, mlp_w1, mlp_w2):
    b, s, d = x.shape
    depth = norm_mix.shape[0]
    n_mixers = 2
    x2 = x.reshape(b * s, d)
    row = lambda v: v.reshape(1, -1).astype(F32)
    lbp = hgrn_lower_bound.astype(F32)

    for layer in range(depth):
        j = layer // n_mixers
        final_gain = row(norm_final) if layer == depth - 1 else None
        if layer % n_mixers == 0:
            q, k, v, lf, sg = _hgrn_pre(x2, row(norm_mix[layer]), hgrn_w_in[j].astype(BF16), lbp, j)
            to3 = lambda a: a.reshape(b, s, d)
            mix = _hgrn_scan(to3(q), to3(k), to3(v), to3(lf), to3(sg), row(hgrn_out_norm[j])).reshape(b * s, d)
            wo = hgrn_w_out[j]
        else:
            qkvs = _attn_qkv(x2, row(norm_mix[layer]), attn_w_qkv[j].astype(BF16))
            outs, lses = [], []
            for qkv, (window, dilation) in zip(qkvs, DILATED_GROUPS):
                assert window // dilation == ATTN_BLOCK and s % (ATTN_BLOCK * dilation) == 0
                o_g, lse_g = _attn_group(qkv, dilation, s)
                outs.append(o_g)
                lses.append(lse_g)
            mix = (*outs, *lses)
            wo = attn_w_out[j]
        x2 = _post(x2, mix, wo.astype(BF16), row(norm_mlp[layer]), mlp_w1[layer].astype(BF16),
                   mlp_w2[layer].astype(BF16), final_gain)
    return x2.reshape(b, s, d)
```

```python
import functools

import numpy as np
import jax
import jax.numpy as jnp
from jax import lax
from jax.experimental import pallas as pl
from jax.experimental.pallas import tpu as pltpu

F32 = jnp.float32
BF16 = jnp.bfloat16

LOG2_E = 1.4426950408889634
LN_2 = 0.6931471805599453

D_MODEL = 1024
NORM_EPS = 1e-6
MLP_HIDDEN = 4 * D_MODEL

HGRN_HEAD_DIM = 128
HGRN_HEADS = D_MODEL // HGRN_HEAD_DIM
HGRN_CHUNK = 128
HGRN_COMPACT_FROM = 16

ATTN_HEADS = 16
ATTN_HEAD_DIM = D_MODEL // ATTN_HEADS
DILATED_GROUPS = ((128, 1), (512, 4), (2048, 16))
N_GROUPS = len(DILATED_GROUPS)
ATTN_BLOCK = 128
ATTN_STEP_BLOCKS = 2
LANES = 128
HEAD_PAIRS = ATTN_HEADS // 2

VMEM_LIMIT_BYTES = 56 * 1024 * 1024

ROW_TILE = 512


def _dot(a, b):
    return jnp.dot(a, b, preferred_element_type=F32)


def _dot_nt(a, b):
    return lax.dot_general(a, b, (((1,), (1,)), ((), ())), preferred_element_type=F32)


def _dot_tn(a, b):
    return lax.dot_general(a, b, (((0,), (0,)), ((), ())), preferred_element_type=F32)


def _split_bf16(x):
    hi = x.astype(BF16)
    return hi, (x - hi.astype(F32)).astype(BF16)


def _rmsnorm_rows(x, gain):
    return x * lax.rsqrt(jnp.mean(x * x, axis=-1, keepdims=True) + NORM_EPS) * gain


def _params(*semantics):
    return pltpu.CompilerParams(dimension_semantics=semantics, vmem_limit_bytes=VMEM_LIMIT_BYTES)


def _const_spec(shape):
    zeros = (0,) * len(shape)
    return pl.BlockSpec(shape, lambda *_: zeros, pipeline_mode=pl.Buffered(1))


def _hgrn_pre_kernel(layer_j, x_ref, gain_ref, w_ref, lbp_ref, q_ref, k_ref, v_ref, lf_ref, sg_ref):
    d = D_MODEL
    h = _rmsnorm_rows(x_ref[...], gain_ref[...]).astype(BF16)

    q = _dot(h, w_ref[:, 0:d])
    q_ref[...] = (q / (1.0 + jnp.exp2(q * (-LOG2_E)))).astype(BF16)

    f = _dot(h, w_ref[:, d:2 * d])
    f2 = f * LOG2_E
    ef = jnp.exp2(-jnp.abs(f2))
    inv = 1.0 / (1.0 + ef)
    log_sig = jnp.minimum(f2, 0.0) - jnp.log2(1.0 + ef)
    sig_neg = jnp.where(f >= 0.0, ef * inv, inv)
    if layer_j == 0:
        lf_ref[...] = log_sig
        k_ref[...] = sig_neg.astype(BF16)
    else:
        p = lbp_ref[...]
        e = jnp.exp(p - jnp.max(p, axis=0, keepdims=True))
        sm = e / jnp.sum(e, axis=0, keepdims=True)
        lb = sm[1:2, :]
        for i in range(2, layer_j + 1):
            lb = lb + sm[i:i + 1, :]
        a = jnp.log2(lb)
        b = jnp.log2(1.0 - lb) + log_sig
        lf_ref[...] = jnp.maximum(a, b) + jnp.log2(1.0 + jnp.exp2(-jnp.abs(a - b)))
        k_ref[...] = ((1.0 - lb) * sig_neg).astype(BF16)

    v_ref[...] = _dot(h, w_ref[:, 2 * d:3 * d]).astype(BF16)

    g = _dot(h, w_ref[:, 3 * d:4 * d])
    sg_ref[...] = (g / (1.0 + jnp.exp2(g * (-LOG2_E)))).astype(BF16)


def _hgrn_pre(x2, gain, w_in, lbp, layer_j):
    t, d = x2.shape
    tm = min(ROW_TILE, t)
    row = lambda i: (i, 0)
    out_bf = jax.ShapeDtypeStruct((t, d), BF16)
    return pl.pallas_call(
        functools.partial(_hgrn_pre_kernel, layer_j),
        grid=(t // tm,),
        in_specs=[pl.BlockSpec((tm, d), row), _const_spec((1, d)), _const_spec((d, 4 * d)),
                  _const_spec(lbp.shape)],
        out_specs=[pl.BlockSpec((tm, d), row)] * 5,
        out_shape=[out_bf, out_bf, out_bf, jax.ShapeDtypeStruct((t, d), F32), out_bf],
        compiler_params=_params("arbitrary"),
        name="hgrn_pre",
    )(x2, gain, w_in, lbp)


def _hgrn_levels(c):
    m, out = 1, []
    while m < c:
        out.append(m)
        m *= 2
    return out


def _hgrn_tables(c):
    levels = _hgrn_levels(c)
    idx = np.arange(c)
    x, j = idx[:, None], idx[None, :]
    blocks = [(j <= x), (j > x)]
    masks = [(x == j)]
    for m in levels:
        bnd = (x // (2 * m)) * (2 * m) + m - 1
        second = (x % (2 * m)) >= m
        blocks.append(np.where(second, (j > bnd) & (j <= x), (j > x) & (j <= bnd)))
        t, s = x, j
        mask = (t // (2 * m) == s // (2 * m)) & ((t % (2 * m)) >= m) & ((s % (2 * m)) < m)
        if m < HGRN_COMPACT_FROM:
            masks.append(mask)
        elif c // (2 * m) > 1:
            masks.append(mask[second[:, 0]])
    a = np.concatenate(blocks, axis=0).astype(np.float32)
    a2 = np.concatenate([a, a], axis=1)
    masks = np.concatenate(masks, axis=0).astype(np.float32)
    masks2 = np.concatenate([masks, masks], axis=1)
    return jnp.asarray(a2, BF16), jnp.asarray(masks2)


def _hgrn_scan_kernel(q_ref, k_ref, v_ref, lf_ref, sg_ref, a2_ref, mask_ref, onorm_ref, y_ref,
                      state_ref, g_ref):
    c = HGRN_CHUNK
    hd = HGRN_HEAD_DIM
    levels = _hgrn_levels(c)

    @pl.when(pl.program_id(1) == 0)
    def _():
        state_ref[...] = jnp.zeros_like(state_ref)

    lf_hi, lf_lo = _split_bf16(lf_ref[0])
    g_ref[...] = _dot(a2_ref[...], jnp.concatenate([lf_hi, lf_lo], axis=0))

    zeros = jnp.zeros((c, hd), BF16)
    zeros2 = jnp.zeros((c, 2 * hd), BF16)

    def block_diag(x):
        return jnp.concatenate([jnp.concatenate([x[:, :hd], zeros[:x.shape[0]]], axis=1),
                                jnp.concatenate([zeros[:x.shape[0]], x[:, hd:]], axis=1)], axis=0)

    def decayed(x_bf, block, rows=slice(0, c)):
        e = jnp.exp2(g_ref[block * c + rows.start:block * c + rows.stop, sl])
        return x_bf[rows] * e.astype(BF16)

    tile = 16
    for p in range(HGRN_HEADS // 2):
        sl = slice(2 * p * hd, 2 * (p + 1) * hd)
        q_bf = q_ref[0, :, sl]
        k_bf = k_ref[0, :, sl]
        v_bf = v_ref[0, :, sl]
        st = [state_ref[2 * p], state_ref[2 * p + 1]]
        st_bd = block_diag(jnp.concatenate([st[0].astype(BF16), st[1].astype(BF16)], axis=1))

        o = _dot_nt(decayed(q_bf, 0), st_bd)

        pieces = [None] * (c // tile)

        def add_rows(start, val):
            for i in range(val.shape[0] // tile):
                piece = val[i * tile:(i + 1) * tile]
                at = start // tile + i
                pieces[at] = piece if pieces[at] is None else pieces[at] + piece

        add_rows(0, mask_ref[0:c] * _dot_nt(q_bf, block_diag(k_bf)))
        mrow = c
        for li, m in enumerate(levels):
            blk = 2 + li
            if m < HGRN_COMPACT_FROM:
                s = _dot_nt(decayed(q_bf, blk), block_diag(decayed(k_bf, blk)))
                add_rows(0, mask_ref[mrow:mrow + c] * s)
                mrow += c
            else:
                n_blk = c // (2 * m)
                first = [slice(b * 2 * m, b * 2 * m + m) for b in range(n_blk)]
                second = [slice(b * 2 * m + m, (b + 1) * 2 * m) for b in range(n_blk)]
                lhs = jnp.concatenate([decayed(q_bf, blk, r) for r in second], axis=0)
                keys = jnp.concatenate([x for r in first for x in (decayed(k_bf, blk, r), zeros2[:m])], axis=0)
                s = _dot_nt(lhs, block_diag(keys))
                if n_blk > 1:
                    s = mask_ref[mrow:mrow + c // 2] * s
                    mrow += c // 2
                for b, r in enumerate(second):
                    add_rows(r.start, s[b * m:(b + 1) * m])
        scores = jnp.concatenate(pieces, axis=0)
        o = o + _dot(scores.astype(BF16), block_diag(v_bf))

        d_state = _dot_tn(v_bf, decayed(k_bf, 1))
        decay_all = jnp.exp2(g_ref[c - 1:c, sl])
        for i in range(2):
            hs = slice(i * hd, (i + 1) * hd)
            state_ref[2 * p + i] = st[i] * decay_all[:, hs] + d_state[hs, hs]
            oh = o[:, hs]
            oh = oh * lax.rsqrt(jnp.mean(oh * oh, axis=-1, keepdims=True) + NORM_EPS * hd)
            lanes = slice((2 * p + i) * hd, (2 * p + i + 1) * hd)
            y_ref[0, :, lanes] = (oh * onorm_ref[:, lanes] * sg_ref[0, :, lanes].astype(F32)).astype(BF16)


def _hgrn_scan(q, k, v, lf, sg, out_norm):
    b, s, d = q.shape
    c = HGRN_CHUNK
    a2, masks = _hgrn_tables(c)
    tok = pl.BlockSpec((1, c, d), lambda i, j: (i, j, 0))
    return pl.pallas_call(
        _hgrn_scan_kernel,
        grid=(b, s // c),
        in_specs=[tok, tok, tok, tok, tok, _const_spec(a2.shape), _const_spec(masks.shape),
                  _const_spec((1, d))],
        out_specs=tok,
        out_shape=jax.ShapeDtypeStruct((b, s, d), BF16),
        scratch_shapes=[pltpu.VMEM((HGRN_HEADS, HGRN_HEAD_DIM, HGRN_HEAD_DIM), F32),
                        pltpu.VMEM((a2.shape[0], d), F32)],
        compiler_params=_params("arbitrary", "arbitrary"),
        name="hgrn_scan",
    )(q, k, v, lf, sg, a2, masks, out_norm)


def _tile_perm(dilation):
    n = ROW_TILE // dilation
    p = np.zeros((ROW_TILE, ROW_TILE), np.float32)
    for r in range(dilation):
        for m in range(n):
            p[r * n + m, m * dilation + r] = 1.0
    return p


def _span_tiles(dilation):
    return max(1, ATTN_BLOCK * dilation // ROW_TILE)


def _grouped_view(a, dilation):
    t, n = a.shape
    k = _span_tiles(dilation)
    if k == 1:
        return a
    return a.reshape(t // (k * ROW_TILE), dilation, k, ROW_TILE // dilation, n)


def _grouped_spec(n, dilation):
    k = _span_tiles(dilation)
    if k == 1:
        return pl.BlockSpec((ROW_TILE, n), lambda i, *_: (i, 0))
    return pl.BlockSpec((None, dilation, None, ROW_TILE // dilation, n), lambda i, *_: (i // k, 0, i % k, 0, 0))


def _grouped_store(ref, dilation, val):
    if _span_tiles(dilation) == 1:
        ref[...] = val
    else:
        n = ROW_TILE // dilation
        for r in range(dilation):
            ref[r] = val[r * n:(r + 1) * n]


def _grouped_load(ref, dilation):
    if _span_tiles(dilation) == 1:
        return ref[...]
    return jnp.concatenate([ref[r] for r in range(dilation)], axis=0)


def _attn_qkv_kernel(dilation, x_ref, gain_ref, *rest):
    d = D_MODEL
    h = _rmsnorm_rows(x_ref[...], gain_ref[...]).astype(BF16)
    if dilation > 1:
        perm_ref, w_ref, o_ref = rest
        h = _dot(perm_ref[...], h).astype(BF16)
    else:
        w_ref, o_ref = rest
    res = _dot(h, w_ref[...])
    val = jnp.concatenate([(res[:, :d] * (ATTN_HEAD_DIM ** -0.5 * LOG2_E)).astype(BF16),
                           res[:, d:].astype(BF16)], axis=1)
    _grouped_store(o_ref, dilation, val)


def _attn_qkv(x2, gain, w_g, dilation):
    t, d = x2.shape
    n = w_g.shape[1]
    perm = [jnp.asarray(_tile_perm(dilation), BF16)] if dilation > 1 else []
    out = pl.pallas_call(
        functools.partial(_attn_qkv_kernel, dilation),
        grid=(t // ROW_TILE,),
        in_specs=[pl.BlockSpec((ROW_TILE, d), lambda i: (i, 0)), _const_spec((1, d))]
        + [_const_spec(p.shape) for p in perm] + [_const_spec(w_g.shape)],
        out_specs=_grouped_spec(n, dilation),
        out_shape=jax.ShapeDtypeStruct(_grouped_view(jnp.empty((t, n), BF16), dilation).shape, BF16),
        compiler_params=_params("arbitrary"),
        name=f"attn_qkv_d{dilation}",
    )(x2, gain, *perm, w_g)
    return out.reshape(t, n)


def _attn_group_kernel(dilation, blocks_per_seq, q_ref, kp_ref, kc_ref, vp_ref, vc_ref, lm_ref, sel_ref, bias_ref,
                       o_ref, lse_ref):
    blk = ATTN_BLOCK
    nb = ATTN_STEP_BLOCKS
    m_lo, m_hi = lm_ref[0], lm_ref[1]
    low_q = lax.broadcasted_iota(jnp.int32, (blk, LANES), 1) < ATTN_HEAD_DIM

    for j in range(nb):
        rows = slice(j * blk, (j + 1) * blk)
        if dilation >= nb:
            prev_k, prev_v, prev_rows = kp_ref, vp_ref, rows
        elif j == 0:
            prev_k, prev_v, prev_rows = kp_ref, vp_ref, slice((nb - 1) * blk, nb * blk)
        else:
            prev_k, prev_v, prev_rows = kc_ref, vc_ref, slice((j - 1) * blk, j * blk)
        has_prev = (((pl.program_id(0) * nb + j) % blocks_per_seq) >= dilation).astype(jnp.int32)
        m_tile = jnp.zeros((blk, LANES), F32)
        l_tile = sel_ref[HEAD_PAIRS]

        for p in range(HEAD_PAIRS):
            sl = slice(p * LANES, (p + 1) * LANES)
            qp = q_ref[rows, sl]
            q_bd = jnp.concatenate([qp * m_lo[:blk], qp * m_hi[:blk]], axis=0)
            k_cat = jnp.concatenate([prev_k[prev_rows, sl], kc_ref[rows, sl]], axis=0)
            s = _dot_nt(q_bd, k_cat) + bias_ref[has_prev]
            s0, s1 = s[:blk], s[blk:]
            m0 = jnp.max(s0, axis=-1, keepdims=True)
            m1 = jnp.max(s1, axis=-1, keepdims=True)
            p_cat = jnp.concatenate([jnp.exp2(s0 - m0).astype(BF16), jnp.exp2(s1 - m1).astype(BF16)], axis=1)
            v_cat = jnp.concatenate([prev_v[prev_rows, sl], vc_ref[rows, sl]], axis=0)
            rhs = jnp.concatenate([jnp.concatenate([v_cat * m_lo, m_lo], axis=1),
                                   jnp.concatenate([v_cat * m_hi, m_hi], axis=1)], axis=0)
            res = _dot(p_cat, rhs)
            denom = res[:, LANES:]
            o_ref[rows, sl] = (res[:, :LANES] / denom).astype(o_ref.dtype)
            m_tile = m_tile + sel_ref[p] * jnp.where(low_q, m0, m1)
            l_tile = l_tile + sel_ref[p] * denom

        lse_ref[rows, :] = m_tile * LN_2 + jnp.log(l_tile)


def _attn_group(qkv, dilation, seq):
    t, n = qkv.shape
    d = D_MODEL
    nb = ATTN_STEP_BLOCKS
    blk = nb * ATTN_BLOCK
    assert dilation == 1 or dilation % nb == 0
    back = max(dilation // nb, 1)

    def spec(which, prev):
        if prev:
            return pl.BlockSpec((blk, d), lambda i: (jnp.maximum(i - back, 0), which))
        return pl.BlockSpec((blk, d), lambda i: (i, which))

    first = np.broadcast_to(np.arange(LANES) < ATTN_HEAD_DIM, (2 * ATTN_BLOCK, LANES))
    lane_masks = jnp.asarray(np.stack([first, ~first]).astype(np.float32), BF16)
    select = np.zeros((HEAD_PAIRS + 1, ATTN_BLOCK, LANES), np.float32)
    for p in range(HEAD_PAIRS):
        select[p, :, [p, ATTN_HEAD_DIM + p]] = 1.0
    select[HEAD_PAIRS] = 1.0 - select[:HEAD_PAIRS].sum(axis=0)
    lane_select = jnp.asarray(select)
    a = np.arange(2 * ATTN_BLOCK)[:, None] % ATTN_BLOCK
    c = np.arange(2 * ATTN_BLOCK)[None, :]
    own = (c >= ATTN_BLOCK) & (c - ATTN_BLOCK <= a)
    earlier = (c < ATTN_BLOCK) & (c >= a)
    bias = jnp.asarray(np.where(np.stack([own, own | earlier]), 0.0, -np.inf).astype(np.float32))

    return pl.pallas_call(
        functools.partial(_attn_group_kernel, dilation, seq // ATTN_BLOCK),
        grid=(t // blk,),
        in_specs=[spec(0, False), spec(1, True), spec(1, False), spec(2, True), spec(2, False),
                  _const_spec(lane_masks.shape), _const_spec(lane_select.shape), _const_spec(bias.shape)],
        out_specs=[pl.BlockSpec((blk, d), lambda i: (i, 0)), pl.BlockSpec((blk, LANES), lambda i: (i, 0))],
        out_shape=[jax.ShapeDtypeStruct((t, d), BF16), jax.ShapeDtypeStruct((t, LANES), F32)],
        compiler_params=_params("arbitrary"),
        name=f"attn_group_d{dilation}",
    )(qkv, qkv, qkv, qkv, qkv, lane_masks, lane_select, bias)


def _head_expand_matrix():
    e = np.zeros((LANES, D_MODEL), np.float32)
    for h in range(ATTN_HEADS):
        lane = (h // 2) + ATTN_HEAD_DIM * (h % 2)
        e[lane, h * ATTN_HEAD_DIM:(h + 1) * ATTN_HEAD_DIM] = 1.0
    return jnp.asarray(e, BF16)


def _mlp_tail(x1, gain_ref, w1_ref, w2_ref, final_gain_ref, o_ref):
    d = D_MODEL
    h = _rmsnorm_rows(x1, gain_ref[...]).astype(BF16)
    acc = x1
    for c in range(MLP_HIDDEN // d):
        a = jnp.maximum(_dot(h, w1_ref[:, c * d:(c + 1) * d]), 0.0)
        acc = acc + _dot((a * a).astype(BF16), w2_ref[c * d:(c + 1) * d, :])
    if final_gain_ref is not None:
        acc = _rmsnorm_rows(acc, final_gain_ref[...])
    o_ref[...] = acc


def _post_kernel(final, x_ref, y_ref, wo_ref, gain_ref, w1_ref, w2_ref, *rest):
    if final:
        final_gain_ref, o_ref = rest
    else:
        final_gain_ref, (o_ref,) = None, rest
    x1 = x_ref[...] + _dot(y_ref[...], wo_ref[...])
    _mlp_tail(x1, gain_ref, w1_ref, w2_ref, final_gain_ref, o_ref)


def _attn_post_kernel(final, x_ref, o0_ref, o1_ref, o2_ref, l0_ref, l1_ref, l2_ref, u1_ref, u2_ref, exp_ref,
                      wo_ref, gain_ref, w1_ref, w2_ref, *rest):
    if final:
        final_gain_ref, o_ref = rest
    else:
        final_gain_ref, (o_ref,) = None, rest
    dils = [dil for _, dil in DILATED_GROUPS]
    outs = [o0_ref[...]]
    lses = [l0_ref[...]]
    for o_g, l_g, u_ref, dil in zip((o1_ref, o2_ref), (l1_ref, l2_ref), (u1_ref, u2_ref), dils[1:]):
        outs.append(_dot(u_ref[...], _grouped_load(o_g, dil)).astype(BF16))
        both = _dot(u_ref[...], jnp.concatenate(_split_bf16(_grouped_load(l_g, dil)), axis=1))
        lses.append(both[:, :LANES] + both[:, LANES:])
    m = jnp.maximum(jnp.maximum(lses[0], lses[1]), lses[2])
    es = [jnp.exp(l - m) for l in lses]
    inv = 1.0 / (es[0] + es[1] + es[2])
    y = None
    for e, o_nat in zip(es, outs):
        term = _dot((e * inv).astype(BF16), exp_ref[...]) * o_nat.astype(F32)
        y = term if y is None else y + term
    x1 = x_ref[...] + _dot(y.astype(BF16), wo_ref[...])
    _mlp_tail(x1, gain_ref, w1_ref, w2_ref, final_gain_ref, o_ref)


def _post(x2, mix, wo, gain, w1, w2, final_gain):
    t, d = x2.shape
    assert t % ROW_TILE == 0
    row = lambda i: (i, 0)
    final = final_gain is not None
    weights = [wo, gain, w1, w2] + ([final_gain] if final else [])
    weight_specs = [_const_spec(w.shape) for w in weights]
    if isinstance(mix, tuple):
        dils = [dil for _, dil in DILATED_GROUPS]
        unperms = [jnp.asarray(_tile_perm(dil).T, BF16) for dil in dils[1:]]
        expand = _head_expand_matrix()
        body = functools.partial(_attn_post_kernel, final)
        views = [_grouped_view(a, dil) for a, dil in zip(mix, dils + dils)]
        args = [x2, *views, *unperms, expand] + weights
        in_specs = ([pl.BlockSpec((ROW_TILE, d), row)] + [_grouped_spec(d, dil) for dil in dils]
                    + [_grouped_spec(LANES, dil) for dil in dils]
                    + [_const_spec(u.shape) for u in unperms] + [_const_spec(expand.shape)] + weight_specs)
    else:
        body = functools.partial(_post_kernel, final)
        args = [x2, mix] + weights
        in_specs = [pl.BlockSpec((ROW_TILE, d), row)] * 2 + weight_specs
    return pl.pallas_call(
        body,
        grid=(t // ROW_TILE,),
        in_specs=in_specs,
        out_specs=pl.BlockSpec((ROW_TILE, d), row),
        out_shape=jax.ShapeDtypeStruct((t, d), F32),
        compiler_params=_params("arbitrary"),
        name="post_mlp",
    )(*args)


def kernel(x, norm_mix, norm_mlp, norm_final, hgrn_w_in, hgrn_lower_bound, hgrn_out_norm, hgrn_w_out,
           attn_w_qkv, attn_w_out, mlp_w1, mlp_w2):
    b, s, d = x.shape
    depth = norm_mix.shape[0]
    n_mixers = 2
    x2 = x.reshape(b * s, d)
    row = lambda v: v.reshape(1, -1).astype(F32)
    lbp = hgrn_lower_bound.astype(F32)

    for layer in range(depth):
        j = layer // n_mixers
        final_gain = row(norm_final) if layer == depth - 1 else None
        if layer % n_mixers == 0:
            q, k, v, lf, sg = _hgrn_pre(x2, row(norm_mix[layer]), hgrn_w_in[j].astype(BF16), lbp, j)
            to3 = lambda a: a.reshape(b, s, d)
            mix = _hgrn_scan(to3(q), to3(k), to3(v), to3(lf), to3(sg), row(hgrn_out_norm[j])).reshape(b * s, d)
            wo = hgrn_w_out[j]
        else:
            w_qkv = attn_w_qkv[j].astype(BF16)
            outs, lses = [], []
            for g, (window, dilation) in enumerate(DILATED_GROUPS):
                assert window // dilation == ATTN_BLOCK and s % (ATTN_BLOCK * dilation) == 0
                qkv = _attn_qkv(x2, row(norm_mix[layer]), w_qkv[:, 3 * d * g:3 * d * (g + 1)], dilation)
                o_g, lse_g = _attn_group(qkv, dilation, s)
                outs.append(o_g)
                lses.append(lse_g)
            mix = (*outs, *lses)
            wo = attn_w_out[j]
        x2 = _post(x2, mix, wo.astype(BF16), row(norm_mlp[layer]), mlp_w1[layer].astype(BF16),
                   mlp_w2[layer].astype(BF16), final_gain)
    return x2.reshape(b, s, d)
```

```python
import functools

import numpy as np
import jax
import jax.numpy as jnp
from jax import lax
from jax.experimental import pallas as pl
from jax.experimental.pallas import tpu as pltpu

F32 = jnp.float32
BF16 = jnp.bfloat16

LOG2_E = 1.4426950408889634
LN_2 = 0.6931471805599453

D_MODEL = 1024
NORM_EPS = 1e-6
MLP_HIDDEN = 4 * D_MODEL

HGRN_HEAD_DIM = 128
HGRN_HEADS = D_MODEL // HGRN_HEAD_DIM
HGRN_CHUNK = 128
HGRN_STEP_CHUNKS = 4
HGRN_COMPACT_FROM = 16

ATTN_HEADS = 16
ATTN_HEAD_DIM = D_MODEL // ATTN_HEADS
DILATED_GROUPS = ((128, 1), (512, 4), (2048, 16))
N_GROUPS = len(DILATED_GROUPS)
ATTN_BLOCK = 128
ATTN_STEP_BLOCKS = 4
LANES = 128
HEAD_PAIRS = ATTN_HEADS // 2

VMEM_LIMIT_BYTES = 56 * 1024 * 1024

ROW_TILE = 512
PRE_COLS = 256


def _dot(a, b):
    return jnp.dot(a, b, preferred_element_type=F32)


def _dot_nt(a, b):
    return lax.dot_general(a, b, (((1,), (1,)), ((), ())), preferred_element_type=F32)


def _dot_tn(a, b):
    return lax.dot_general(a, b, (((0,), (0,)), ((), ())), preferred_element_type=F32)


def _split_bf16(x):
    hi = x.astype(BF16)
    return hi, (x - hi.astype(F32)).astype(BF16)


def _rmsnorm_rows(x, gain):
    return x * lax.rsqrt(jnp.mean(x * x, axis=-1, keepdims=True) + NORM_EPS) * gain


def _params(*semantics):
    return pltpu.CompilerParams(dimension_semantics=semantics, vmem_limit_bytes=VMEM_LIMIT_BYTES)


def _const_spec(shape):
    zeros = (0,) * len(shape)
    return pl.BlockSpec(shape, lambda *_: zeros, pipeline_mode=pl.Buffered(1))


def _hgrn_pre_kernel(layer_j, x_ref, gain_ref, w_ref, lbp_ref, q_ref, k_ref, v_ref, lf_ref, sg_ref):
    d = D_MODEL
    h = _rmsnorm_rows(x_ref[...], gain_ref[...]).astype(BF16)

    if layer_j > 0:
        p = lbp_ref[...]
        e = jnp.exp(p - jnp.max(p, axis=0, keepdims=True))
        sm = e / jnp.sum(e, axis=0, keepdims=True)
        lb = sm[1:2, :]
        for i in range(2, layer_j + 1):
            lb = lb + sm[i:i + 1, :]
        log_lb = jnp.log2(lb)
        log_1m_lb = jnp.log2(1.0 - lb)

    for j in range(d // PRE_COLS):
        out = slice(j * PRE_COLS, (j + 1) * PRE_COLS)
        proj = lambda blk: _dot(h, w_ref[:, blk * d + j * PRE_COLS:blk * d + (j + 1) * PRE_COLS])

        q = proj(0)
        q_ref[:, out] = (q / (1.0 + jnp.exp2(q * (-LOG2_E)))).astype(BF16)

        f = proj(1)
        f2 = f * LOG2_E
        ef = jnp.exp2(-jnp.abs(f2))
        inv = 1.0 / (1.0 + ef)
        log_sig = jnp.minimum(f2, 0.0) - jnp.log2(1.0 + ef)
        sig_neg = jnp.where(f >= 0.0, ef * inv, inv)
        if layer_j == 0:
            lf_ref[:, out] = log_sig
            k_ref[:, out] = sig_neg.astype(BF16)
        else:
            a = log_lb[:, out]
            b = log_1m_lb[:, out] + log_sig
            lf_ref[:, out] = jnp.maximum(a, b) + jnp.log2(1.0 + jnp.exp2(-jnp.abs(a - b)))
            k_ref[:, out] = ((1.0 - lb[:, out]) * sig_neg).astype(BF16)

        v_ref[:, out] = proj(2).astype(BF16)

        g = proj(3)
        sg_ref[:, out] = (g / (1.0 + jnp.exp2(g * (-LOG2_E)))).astype(BF16)


def _hgrn_pre(x2, gain, w_in, lbp, layer_j):
    t, d = x2.shape
    tm = min(ROW_TILE, t)
    row = lambda i: (i, 0)
    out_bf = jax.ShapeDtypeStruct((t, d), BF16)
    return pl.pallas_call(
        functools.partial(_hgrn_pre_kernel, layer_j),
        grid=(t // tm,),
        in_specs=[pl.BlockSpec((tm, d), row), _const_spec((1, d)), _const_spec((d, 4 * d)),
                  _const_spec(lbp.shape)],
        out_specs=[pl.BlockSpec((tm, d), row)] * 5,
        out_shape=[out_bf, out_bf, out_bf, jax.ShapeDtypeStruct((t, d), F32), out_bf],
        compiler_params=_params("arbitrary"),
        name="hgrn_pre",
    )(x2, gain, w_in, lbp)


def _hgrn_levels(c):
    m, out = 1, []
    while m < c:
        out.append(m)
        m *= 2
    return out


def _hgrn_tables(c):
    levels = _hgrn_levels(c)
    idx = np.arange(c)
    x, j = idx[:, None], idx[None, :]
    blocks = [(j <= x), (j > x)]
    masks = [(x == j)]
    for m in levels:
        bnd = (x // (2 * m)) * (2 * m) + m - 1
        second = (x % (2 * m)) >= m
        blocks.append(np.where(second, (j > bnd) & (j <= x), (j > x) & (j <= bnd)))
        t, s = x, j
        mask = (t // (2 * m) == s // (2 * m)) & ((t % (2 * m)) >= m) & ((s % (2 * m)) < m)
        if m < HGRN_COMPACT_FROM:
            masks.append(mask)
        elif c // (2 * m) > 1:
            masks.append(mask[second[:, 0]])
    a = np.concatenate(blocks, axis=0).astype(np.float32)
    a2 = np.concatenate([a, a], axis=1)
    masks = np.concatenate(masks, axis=0).astype(np.float32)
    masks2 = np.concatenate([masks, masks], axis=1)
    return jnp.asarray(a2, BF16), jnp.asarray(masks2)


def _hgrn_scan_kernel(q_ref, k_ref, v_ref, lf_ref, sg_ref, a2_ref, mask_ref, onorm_ref, y_ref,
                      state_ref, g_ref):
    c = HGRN_CHUNK
    hd = HGRN_HEAD_DIM
    levels = _hgrn_levels(c)

    @pl.when(pl.program_id(1) == 0)
    def _():
        state_ref[...] = jnp.zeros_like(state_ref)

    for ci in range(HGRN_STEP_CHUNKS):
        lf_hi, lf_lo = _split_bf16(lf_ref[0, ci * c:(ci + 1) * c])
        g_ref[ci] = _dot(a2_ref[...], jnp.concatenate([lf_hi, lf_lo], axis=0))

    zeros = jnp.zeros((c, hd), BF16)
    zeros2 = jnp.zeros((c, 2 * hd), BF16)

    def block_diag(x):
        return jnp.concatenate([jnp.concatenate([x[:, :hd], zeros[:x.shape[0]]], axis=1),
                                jnp.concatenate([zeros[:x.shape[0]], x[:, hd:]], axis=1)], axis=0)

    def decayed(x_bf, block, rows=slice(0, c)):
        e = jnp.exp2(g_ref[ci, block * c + rows.start:block * c + rows.stop, sl])
        return x_bf[rows] * e.astype(BF16)

    tile = 16
    for ci, p in [(ci, p) for ci in range(HGRN_STEP_CHUNKS) for p in range(HGRN_HEADS // 2)]:
        tok = slice(ci * c, (ci + 1) * c)
        sl = slice(2 * p * hd, 2 * (p + 1) * hd)
        q_bf = q_ref[0, tok, sl]
        k_bf = k_ref[0, tok, sl]
        v_bf = v_ref[0, tok, sl]
        st = [state_ref[2 * p], state_ref[2 * p + 1]]
        st_bd = block_diag(jnp.concatenate([st[0].astype(BF16), st[1].astype(BF16)], axis=1))

        o = _dot_nt(decayed(q_bf, 0), st_bd)

        pieces = [None] * (c // tile)

        def add_rows(start, val):
            for i in range(val.shape[0] // tile):
                piece = val[i * tile:(i + 1) * tile]
                at = start // tile + i
                pieces[at] = piece if pieces[at] is None else pieces[at] + piece

        add_rows(0, mask_ref[0:c] * _dot_nt(q_bf, block_diag(k_bf)))
        mrow = c
        for li, m in enumerate(levels):
            blk = 2 + li
            if m < HGRN_COMPACT_FROM:
                s = _dot_nt(decayed(q_bf, blk), block_diag(decayed(k_bf, blk)))
                add_rows(0, mask_ref[mrow:mrow + c] * s)
                mrow += c
            else:
                n_blk = c // (2 * m)
                first = [slice(b * 2 * m, b * 2 * m + m) for b in range(n_blk)]
                second = [slice(b * 2 * m + m, (b + 1) * 2 * m) for b in range(n_blk)]
                lhs = jnp.concatenate([decayed(q_bf, blk, r) for r in second], axis=0)
                keys = jnp.concatenate([x for r in first for x in (decayed(k_bf, blk, r), zeros2[:m])], axis=0)
                s = _dot_nt(lhs, block_diag(keys))
                if n_blk > 1:
                    s = mask_ref[mrow:mrow + c // 2] * s
                    mrow += c // 2
                for b, r in enumerate(second):
                    add_rows(r.start, s[b * m:(b + 1) * m])
        scores = jnp.concatenate(pieces, axis=0)
        o = o + _dot(scores.astype(BF16), block_diag(v_bf))

        d_state = _dot_tn(v_bf, decayed(k_bf, 1))
        decay_all = jnp.exp2(g_ref[ci, c - 1:c, sl])
        for i in range(2):
            hs = slice(i * hd, (i + 1) * hd)
            state_ref[2 * p + i] = st[i] * decay_all[:, hs] + d_state[hs, hs]
            oh = o[:, hs]
            oh = oh * lax.rsqrt(jnp.mean(oh * oh, axis=-1, keepdims=True) + NORM_EPS * hd)
            lanes = slice((2 * p + i) * hd, (2 * p + i + 1) * hd)
            y_ref[0, tok, lanes] = (oh * onorm_ref[:, lanes] * sg_ref[0, tok, lanes].astype(F32)).astype(BF16)


def _hgrn_scan(q, k, v, lf, sg, out_norm):
    b, s, d = q.shape
    c = HGRN_CHUNK
    a2, masks = _hgrn_tables(c)
    rows = HGRN_STEP_CHUNKS * c
    assert s % rows == 0
    tok = pl.BlockSpec((1, rows, d), lambda i, j: (i, j, 0))
    return pl.pallas_call(
        _hgrn_scan_kernel,
        grid=(b, s // rows),
        in_specs=[tok, tok, tok, tok, tok, _const_spec(a2.shape), _const_spec(masks.shape),
                  _const_spec((1, d))],
        out_specs=tok,
        out_shape=jax.ShapeDtypeStruct((b, s, d), BF16),
        scratch_shapes=[pltpu.VMEM((HGRN_HEADS, HGRN_HEAD_DIM, HGRN_HEAD_DIM), F32),
                        pltpu.VMEM((HGRN_STEP_CHUNKS, a2.shape[0], d), F32)],
        compiler_params=_params("arbitrary", "arbitrary"),
        name="hgrn_scan",
    )(q, k, v, lf, sg, a2, masks, out_norm)


def _tile_perm(dilation):
    n = ROW_TILE // dilation
    p = np.zeros((ROW_TILE, ROW_TILE), np.float32)
    for r in range(dilation):
        for m in range(n):
            p[r * n + m, m * dilation + r] = 1.0
    return p


def _span_tiles(dilation):
    return max(1, ATTN_BLOCK * dilation // ROW_TILE)


def _grouped_view(a, dilation):
    t, n = a.shape
    k = _span_tiles(dilation)
    if k == 1:
        return a
    return a.reshape(t // (k * ROW_TILE), dilation, k, ROW_TILE // dilation, n)


def _grouped_spec(n, dilation):
    k = _span_tiles(dilation)
    if k == 1:
        return pl.BlockSpec((ROW_TILE, n), lambda i, *_: (i, 0))
    return pl.BlockSpec((None, dilation, None, ROW_TILE // dilation, n), lambda i, *_: (i // k, 0, i % k, 0, 0))


def _grouped_store(ref, dilation, val):
    if _span_tiles(dilation) == 1:
        ref[...] = val
    else:
        n = ROW_TILE // dilation
        for r in range(dilation):
            ref[r] = val[r * n:(r + 1) * n]


def _grouped_load(ref, dilation):
    if _span_tiles(dilation) == 1:
        return ref[...]
    return jnp.concatenate([ref[r] for r in range(dilation)], axis=0)


def _attn_qkv_kernel(dilation, x_ref, gain_ref, *rest):
    d = D_MODEL
    h = _rmsnorm_rows(x_ref[...], gain_ref[...]).astype(BF16)
    if dilation > 1:
        perm_ref, w_ref, o_ref = rest
        h = _dot(perm_ref[...], h).astype(BF16)
    else:
        w_ref, o_ref = rest
    res = _dot(h, w_ref[...])
    val = jnp.concatenate([(res[:, :d] * (ATTN_HEAD_DIM ** -0.5 * LOG2_E)).astype(BF16),
                           res[:, d:].astype(BF16)], axis=1)
    _grouped_store(o_ref, dilation, val)


def _attn_qkv(x2, gain, w_g, dilation):
    t, d = x2.shape
    n = w_g.shape[1]
    perm = [jnp.asarray(_tile_perm(dilation), BF16)] if dilation > 1 else []
    out = pl.pallas_call(
        functools.partial(_attn_qkv_kernel, dilation),
        grid=(t // ROW_TILE,),
        in_specs=[pl.BlockSpec((ROW_TILE, d), lambda i: (i, 0)), _const_spec((1, d))]
        + [_const_spec(p.shape) for p in perm] + [_const_spec(w_g.shape)],
        out_specs=_grouped_spec(n, dilation),
        out_shape=jax.ShapeDtypeStruct(_grouped_view(jnp.empty((t, n), BF16), dilation).shape, BF16),
        compiler_params=_params("arbitrary"),
        name=f"attn_qkv_d{dilation}",
    )(x2, gain, *perm, w_g)
    return out.reshape(t, n)


def _attn_group_kernel(dilation, blocks_per_seq, q_ref, kp_ref, kc_ref, vp_ref, vc_ref, lm_ref, sel_ref, bias_ref,
                       o_ref, lse_ref):
    blk = ATTN_BLOCK
    nb = ATTN_STEP_BLOCKS
    m_lo, m_hi = lm_ref[0], lm_ref[1]
    low_q = lax.broadcasted_iota(jnp.int32, (blk, LANES), 1) < ATTN_HEAD_DIM

    for j in range(nb):
        rows = slice(j * blk, (j + 1) * blk)
        if dilation >= nb:
            prev_k, prev_v, prev_rows = kp_ref, vp_ref, rows
        elif j == 0:
            prev_k, prev_v, prev_rows = kp_ref, vp_ref, slice((nb - 1) * blk, nb * blk)
        else:
            prev_k, prev_v, prev_rows = kc_ref, vc_ref, slice((j - 1) * blk, j * blk)
        has_prev = (((pl.program_id(0) * nb + j) % blocks_per_seq) >= dilation).astype(jnp.int32)
        m_tile = jnp.zeros((blk, LANES), F32)
        l_tile = sel_ref[HEAD_PAIRS]

        for p in range(HEAD_PAIRS):
            sl = slice(p * LANES, (p + 1) * LANES)
            qp = q_ref[rows, sl]
            q_bd = jnp.concatenate([qp * m_lo[:blk], qp * m_hi[:blk]], axis=0)
            k_cat = jnp.concatenate([prev_k[prev_rows, sl], kc_ref[rows, sl]], axis=0)
            s = _dot_nt(q_bd, k_cat) + bias_ref[has_prev]
            s0, s1 = s[:blk], s[blk:]
            m0 = jnp.max(s0, axis=-1, keepdims=True)
            m1 = jnp.max(s1, axis=-1, keepdims=True)
            p_cat = jnp.concatenate([jnp.exp2(s0 - m0).astype(BF16), jnp.exp2(s1 - m1).astype(BF16)], axis=1)
            v_cat = jnp.concatenate([prev_v[prev_rows, sl], vc_ref[rows, sl]], axis=0)
            rhs = jnp.concatenate([jnp.concatenate([v_cat * m_lo, m_lo], axis=1),
                                   jnp.concatenate([v_cat * m_hi, m_hi], axis=1)], axis=0)
            res = _dot(p_cat, rhs)
            denom = res[:, LANES:]
            o_ref[rows, sl] = (res[:, :LANES] / denom).astype(o_ref.dtype)
            m_tile = m_tile + sel_ref[p] * jnp.where(low_q, m0, m1)
            l_tile = l_tile + sel_ref[p] * denom

        lse_ref[rows, :] = m_tile * LN_2 + jnp.log(l_tile)


def _attn_group(qkv, dilation, seq):
    t, n = qkv.shape
    d = D_MODEL
    nb = ATTN_STEP_BLOCKS
    blk = nb * ATTN_BLOCK
    assert dilation == 1 or dilation % nb == 0
    back = max(dilation // nb, 1)

    def spec(which, prev):
        if prev:
            return pl.BlockSpec((blk, d), lambda i: (jnp.maximum(i - back, 0), which))
        return pl.BlockSpec((blk, d), lambda i: (i, which))

    first = np.broadcast_to(np.arange(LANES) < ATTN_HEAD_DIM, (2 * ATTN_BLOCK, LANES))
    lane_masks = jnp.asarray(np.stack([first, ~first]).astype(np.float32), BF16)
    select = np.zeros((HEAD_PAIRS + 1, ATTN_BLOCK, LANES), np.float32)
    for p in range(HEAD_PAIRS):
        select[p, :, [p, ATTN_HEAD_DIM + p]] = 1.0
    select[HEAD_PAIRS] = 1.0 - select[:HEAD_PAIRS].sum(axis=0)
    lane_select = jnp.asarray(select)
    a = np.arange(2 * ATTN_BLOCK)[:, None] % ATTN_BLOCK
    c = np.arange(2 * ATTN_BLOCK)[None, :]
    own = (c >= ATTN_BLOCK) & (c - ATTN_BLOCK <= a)
    earlier = (c < ATTN_BLOCK) & (c >= a)
    bias = jnp.asarray(np.where(np.stack([own, own | earlier]), 0.0, -np.inf).astype(np.float32))

    return pl.pallas_call(
        functools.partial(_attn_group_kernel, dilation, seq // ATTN_BLOCK),
        grid=(t // blk,),
        in_specs=[spec(0, False), spec(1, True), spec(1, False), spec(2, True), spec(2, False),
                  _const_spec(lane_masks.shape), _const_spec(lane_select.shape), _const_spec(bias.shape)],
        out_specs=[pl.BlockSpec((blk, d), lambda i: (i, 0)), pl.BlockSpec((blk, LANES), lambda i: (i, 0))],
        out_shape=[jax.ShapeDtypeStruct((t, d), BF16), jax.ShapeDtypeStruct((t, LANES), F32)],
        compiler_params=_params("arbitrary"),
        name=f"attn_group_d{dilation}",
    )(qkv, qkv, qkv, qkv, qkv, lane_masks, lane_select, bias)


def _head_expand_matrix():
    e = np.zeros((LANES, D_MODEL), np.float32)
    for h in range(ATTN_HEADS):
        lane = (h // 2) + ATTN_HEAD_DIM * (h % 2)
        e[lane, h * ATTN_HEAD_DIM:(h + 1) * ATTN_HEAD_DIM] = 1.0
    return jnp.asarray(e, BF16)


def _mlp_tail(x1, gain_ref, w1_ref, w2_ref, final_gain_ref, o_ref):
    d = D_MODEL
    h = _rmsnorm_rows(x1, gain_ref[...]).astype(BF16)
    acc = x1
    for c in range(MLP_HIDDEN // d):
        a = jnp.maximum(_dot(h, w1_ref[:, c * d:(c + 1) * d]), 0.0)
        acc = acc + _dot((a * a).astype(BF16), w2_ref[c * d:(c + 1) * d, :])
    if final_gain_ref is not None:
        acc = _rmsnorm_rows(acc, final_gain_ref[...])
    o_ref[...] = acc


def _post_kernel(final, x_ref, y_ref, wo_ref, gain_ref, w1_ref, w2_ref, *rest):
    if final:
        final_gain_ref, o_ref = rest
    else:
        final_gain_ref, (o_ref,) = None, rest
    x1 = x_ref[...] + _dot(y_ref[...], wo_ref[...])
    _mlp_tail(x1, gain_ref, w1_ref, w2_ref, final_gain_ref, o_ref)


def _attn_post_kernel(final, x_ref, o0_ref, o1_ref, o2_ref, l0_ref, l1_ref, l2_ref, u1_ref, u2_ref, exp_ref,
                      wo_ref, gain_ref, w1_ref, w2_ref, *rest):
    if final:
        final_gain_ref, o_ref = rest
    else:
        final_gain_ref, (o_ref,) = None, rest
    dils = [dil for _, dil in DILATED_GROUPS]
    outs = [o0_ref[...]]
    lses = [l0_ref[...]]
    for o_g, l_g, u_ref, dil in zip((o1_ref, o2_ref), (l1_ref, l2_ref), (u1_ref, u2_ref), dils[1:]):
        outs.append(_dot(u_ref[...], _grouped_load(o_g, dil)).astype(BF16))
        both = _dot(u_ref[...], jnp.concatenate(_split_bf16(_grouped_load(l_g, dil)), axis=1))
        lses.append(both[:, :LANES] + both[:, LANES:])
    m = jnp.maximum(jnp.maximum(lses[0], lses[1]), lses[2])
    es = [jnp.exp(l - m) for l in lses]
    inv = 1.0 / (es[0] + es[1] + es[2])
    y = None
    for e, o_nat in zip(es, outs):
        term = _dot((e * inv).astype(BF16), exp_ref[...]) * o_nat.astype(F32)
        y = term if y is None else y + term
    x1 = x_ref[...] + _dot(y.astype(BF16), wo_ref[...])
    _mlp_tail(x1, gain_ref, w1_ref, w2_ref, final_gain_ref, o_ref)


def _post(x2, mix, wo, gain, w1, w2, final_gain):
    t, d = x2.shape
    assert t % ROW_TILE == 0
    row = lambda i: (i, 0)
    final = final_gain is not None
    weights = [wo, gain, w1, w2] + ([final_gain] if final else [])
    weight_specs = [_const_spec(w.shape) for w in weights]
    if isinstance(mix, tuple):
        dils = [dil for _, dil in DILATED_GROUPS]
        unperms = [jnp.asarray(_tile_perm(dil).T, BF16) for dil in dils[1:]]
        expand = _head_expand_matrix()
        body = functools.partial(_attn_post_kernel, final)
        views = [_grouped_view(a, dil) for a, dil in zip(mix, dils + dils)]
        args = [x2, *views, *unperms, expand] + weights
        in_specs = ([pl.BlockSpec((ROW_TILE, d), row)] + [_grouped_spec(d, dil) for dil in dils]
                    + [_grouped_spec(LANES, dil) for dil in dils]
                    + [_const_spec(u.shape) for u in unperms] + [_const_spec(expand.shape)] + weight_specs)
    else:
        body = functools.partial(_post_kernel, final)
        args = [x2, mix] + weights
        in_specs = [pl.BlockSpec((ROW_TILE, d), row)] * 2 + weight_specs
    return pl.pallas_call(
        body,
        grid=(t // ROW_TILE,),
        in_specs=in_specs,
        out_specs=pl.BlockSpec((ROW_TILE, d), row),
        out_shape=jax.ShapeDtypeStruct((t, d), F32),
        compiler_params=_params("arbitrary"),
        name="post_mlp",
    )(*args)


def kernel(x, norm_mix, norm_mlp, norm_final, hgrn_w_in, hgrn_lower_bound, hgrn_out_norm, hgrn_w_out,
           attn_w_qkv, attn_w_out, mlp_w1, mlp_w2):
    b, s, d = x.shape
    depth = norm_mix.shape[0]
    n_mixers = 2
    x2 = x.reshape(b * s, d)
    row = lambda v: v.reshape(1, -1).astype(F32)
    lbp = hgrn_lower_bound.astype(F32)

    for layer in range(depth):
        j = layer // n_mixers
        final_gain = row(norm_final) if layer == depth - 1 else None
        if layer % n_mixers == 0:
            q, k, v, lf, sg = _hgrn_pre(x2, row(norm_mix[layer]), hgrn_w_in[j].astype(BF16), lbp, j)
            to3 = lambda a: a.reshape(b, s, d)
            mix = _hgrn_scan(to3(q), to3(k), to3(v), to3(lf), to3(sg), row(hgrn_out_norm[j])).reshape(b * s, d)
            wo = hgrn_w_out[j]
        else:
            w_qkv = attn_w_qkv[j].astype(BF16)
            outs, lses = [], []
            for g, (window, dilation) in enumerate(DILATED_GROUPS):
                assert window // dilation == ATTN_BLOCK and s % (ATTN_BLOCK * dilation) == 0
                qkv = _attn_qkv(x2, row(norm_mix[layer]), w_qkv[:, 3 * d * g:3 * d * (g + 1)], dilation)
                o_g, lse_g = _attn_group(qkv, dilation, s)
                outs.append(o_g)
                lses.append(lse_g)
            mix = (*outs, *lses)
            wo = attn_w_out[j]
        x2 = _post(x2, mix, wo.astype(BF16), row(norm_mlp[layer]), mlp_w1[layer].astype(BF16),
                   mlp_w2[layer].astype(BF16), final_gain)
    return x2.reshape(b, s, d)
```

```python
import functools

import numpy as np
import jax
import jax.numpy as jnp
from jax import lax
from jax.experimental import pallas as pl
from jax.experimental.pallas import tpu as pltpu

F32 = jnp.float32
BF16 = jnp.bfloat16

LOG2_E = 1.4426950408889634
LN_2 = 0.6931471805599453

D_MODEL = 1024
NORM_EPS = 1e-6
MLP_HIDDEN = 4 * D_MODEL

HGRN_HEAD_DIM = 128
HGRN_HEADS = D_MODEL // HGRN_HEAD_DIM
HGRN_CHUNK = 128
HGRN_STEP_CHUNKS = 4
HGRN_HALF_ROWS = 16
HGRN_COMPACT_FROM = 16

ATTN_HEADS = 16
ATTN_HEAD_DIM = D_MODEL // ATTN_HEADS
DILATED_GROUPS = ((128, 1), (512, 4), (2048, 16))
N_GROUPS = len(DILATED_GROUPS)
ATTN_BLOCK = 128
ATTN_STEP_BLOCKS = 4
LANES = 128
HEAD_PAIRS = ATTN_HEADS // 2

VMEM_LIMIT_BYTES = 56 * 1024 * 1024

ROW_TILE = 512
PRE_COLS = 256


def _dot(a, b):
    return jnp.dot(a, b, preferred_element_type=F32)


def _dot_nt(a, b):
    return lax.dot_general(a, b, (((1,), (1,)), ((), ())), preferred_element_type=F32)


def _dot_tn(a, b):
    return lax.dot_general(a, b, (((0,), (0,)), ((), ())), preferred_element_type=F32)


def _split_bf16(x):
    hi = x.astype(BF16)
    return hi, (x - hi.astype(F32)).astype(BF16)


def _rmsnorm_rows(x, gain):
    return x * lax.rsqrt(jnp.mean(x * x, axis=-1, keepdims=True) + NORM_EPS) * gain


def _params(*semantics):
    return pltpu.CompilerParams(dimension_semantics=semantics, vmem_limit_bytes=VMEM_LIMIT_BYTES)


def _const_spec(shape):
    zeros = (0,) * len(shape)
    return pl.BlockSpec(shape, lambda *_: zeros, pipeline_mode=pl.Buffered(1))


def _hgrn_pre_kernel(layer_j, x_ref, gain_ref, w_ref, lbp_ref, q_ref, k_ref, v_ref, lf_ref, sg_ref):
    d = D_MODEL
    h = _rmsnorm_rows(x_ref[...], gain_ref[...]).astype(BF16)

    if layer_j > 0:
        p = lbp_ref[...]
        e = jnp.exp(p - jnp.max(p, axis=0, keepdims=True))
        sm = e / jnp.sum(e, axis=0, keepdims=True)
        lb = sm[1:2, :]
        for i in range(2, layer_j + 1):
            lb = lb + sm[i:i + 1, :]
        log_lb = jnp.log2(lb)
        log_1m_lb = jnp.log2(1.0 - lb)

    for j in range(d // PRE_COLS):
        out = slice(j * PRE_COLS, (j + 1) * PRE_COLS)
        proj = lambda blk: _dot(h, w_ref[:, blk * d + j * PRE_COLS:blk * d + (j + 1) * PRE_COLS])

        q = proj(0)
        q_ref[:, out] = (q / (1.0 + jnp.exp2(q * (-LOG2_E)))).astype(BF16)

        f = proj(1)
        f2 = f * LOG2_E
        ef = jnp.exp2(-jnp.abs(f2))
        inv = 1.0 / (1.0 + ef)
        log_sig = jnp.minimum(f2, 0.0) - jnp.log2(1.0 + ef)
        sig_neg = jnp.where(f >= 0.0, ef * inv, inv)
        if layer_j == 0:
            lf_ref[:, out] = log_sig
            k_ref[:, out] = sig_neg.astype(BF16)
        else:
            a = log_lb[:, out]
            b = log_1m_lb[:, out] + log_sig
            lf_ref[:, out] = jnp.maximum(a, b) + jnp.log2(1.0 + jnp.exp2(-jnp.abs(a - b)))
            k_ref[:, out] = ((1.0 - lb[:, out]) * sig_neg).astype(BF16)

        v_ref[:, out] = proj(2).astype(BF16)

        g = proj(3)
        sg_ref[:, out] = (g / (1.0 + jnp.exp2(g * (-LOG2_E)))).astype(BF16)


def _hgrn_pre(x2, gain, w_in, lbp, layer_j):
    t, d = x2.shape
    tm = min(ROW_TILE, t)
    row = lambda i: (i, 0)
    out_bf = jax.ShapeDtypeStruct((t, d), BF16)
    return pl.pallas_call(
        functools.partial(_hgrn_pre_kernel, layer_j),
        grid=(t // tm,),
        in_specs=[pl.BlockSpec((tm, d), row), _const_spec((1, d)), _const_spec((d, 4 * d)),
                  _const_spec(lbp.shape)],
        out_specs=[pl.BlockSpec((tm, d), row)] * 5,
        out_shape=[out_bf, out_bf, out_bf, jax.ShapeDtypeStruct((t, d), F32), out_bf],
        compiler_params=_params("arbitrary"),
        name="hgrn_pre",
    )(x2, gain, w_in, lbp)


def _hgrn_levels(c):
    m, out = 1, []
    while m < c:
        out.append(m)
        m *= 2
    return out


def _hgrn_tables(c):
    levels = _hgrn_levels(c)
    idx = np.arange(c)
    x, j = idx[:, None], idx[None, :]
    blocks = []
    masks = [(x == j)]
    for m in levels:
        bnd = (x // (2 * m)) * (2 * m) + m - 1
        second = (x % (2 * m)) >= m
        blocks.append(np.where(second, (j > bnd) & (j <= x), (j > x) & (j <= bnd)))
        t, s = x, j
        mask = (t // (2 * m) == s // (2 * m)) & ((t % (2 * m)) >= m) & ((s % (2 * m)) < m)
        if m < HGRN_COMPACT_FROM:
            masks.append(mask)
        elif c // (2 * m) > 1:
            masks.append(mask[second[:, 0]])
    blocks.append(np.broadcast_to(idx[None, :] < c // 2, (HGRN_HALF_ROWS, c)))
    a = np.concatenate(blocks, axis=0).astype(np.float32)
    a2 = np.concatenate([a, a], axis=1)
    masks = np.concatenate(masks, axis=0).astype(np.float32)
    masks2 = np.concatenate([masks, masks], axis=1)
    return jnp.asarray(a2, BF16), jnp.asarray(masks2)


def _hgrn_scan_kernel(q_ref, k_ref, v_ref, lf_ref, sg_ref, a2_ref, mask_ref, onorm_ref, y_ref,
                      state_ref, g_ref):
    c = HGRN_CHUNK
    hd = HGRN_HEAD_DIM
    levels = _hgrn_levels(c)

    @pl.when(pl.program_id(1) == 0)
    def _():
        state_ref[...] = jnp.zeros_like(state_ref)

    for ci in range(HGRN_STEP_CHUNKS):
        lf_hi, lf_lo = _split_bf16(lf_ref[0, ci * c:(ci + 1) * c])
        g_ref[ci] = _dot(a2_ref[...], jnp.concatenate([lf_hi, lf_lo], axis=0))

    zeros = jnp.zeros((c, hd), BF16)
    zeros2 = jnp.zeros((c, 2 * hd), BF16)

    def block_diag(x):
        return jnp.concatenate([jnp.concatenate([x[:, :hd], zeros[:x.shape[0]]], axis=1),
                                jnp.concatenate([zeros[:x.shape[0]], x[:, hd:]], axis=1)], axis=0)

    def scaled(x_bf, log2_decay):
        return x_bf * jnp.exp2(log2_decay).astype(BF16)

    def decayed(x_bf, block, rows=slice(0, c)):
        return scaled(x_bf[rows], g_ref[ci, block * c + rows.start:block * c + rows.stop, sl])

    tile = 16
    half = c // 2
    for ci, p in [(ci, p) for ci in range(HGRN_STEP_CHUNKS) for p in range(HGRN_HEADS // 2)]:
        tok = slice(ci * c, (ci + 1) * c)
        sl = slice(2 * p * hd, 2 * (p + 1) * hd)
        q_bf = q_ref[0, tok, sl]
        k_bf = k_ref[0, tok, sl]
        v_bf = v_ref[0, tok, sl]
        st = [state_ref[2 * p], state_ref[2 * p + 1]]
        st_bd = block_diag(jnp.concatenate([st[0].astype(BF16), st[1].astype(BF16)], axis=1))

        top = (len(levels) - 1) * c
        g_lo = g_ref[ci, top:top + half, sl]
        g_hi = g_ref[ci, top + half:top + c, sl]
        first_half = g_ref[ci, len(levels) * c:len(levels) * c + 1, sl]
        g_last = g_hi[half - 1:half]
        since_start = jnp.concatenate([first_half - g_lo, first_half + g_hi], axis=0)
        until_end = jnp.concatenate([g_last + g_lo, g_last - g_hi], axis=0)

        o = _dot_nt(scaled(q_bf, since_start), st_bd)

        pieces = [None] * (c // tile)

        def add_rows(start, val):
            for i in range(val.shape[0] // tile):
                piece = val[i * tile:(i + 1) * tile]
                at = start // tile + i
                pieces[at] = piece if pieces[at] is None else pieces[at] + piece

        add_rows(0, mask_ref[0:c] * _dot_nt(q_bf, block_diag(k_bf)))
        mrow = c
        for li, m in enumerate(levels):
            blk = li
            if m < HGRN_COMPACT_FROM:
                s = _dot_nt(decayed(q_bf, blk), block_diag(decayed(k_bf, blk)))
                add_rows(0, mask_ref[mrow:mrow + c] * s)
                mrow += c
            else:
                n_blk = c // (2 * m)
                first = [slice(b * 2 * m, b * 2 * m + m) for b in range(n_blk)]
                second = [slice(b * 2 * m + m, (b + 1) * 2 * m) for b in range(n_blk)]
                lhs = jnp.concatenate([decayed(q_bf, blk, r) for r in second], axis=0)
                keys = jnp.concatenate([x for r in first for x in (decayed(k_bf, blk, r), zeros2[:m])], axis=0)
                s = _dot_nt(lhs, block_diag(keys))
                if n_blk > 1:
                    s = mask_ref[mrow:mrow + c // 2] * s
                    mrow += c // 2
                for b, r in enumerate(second):
                    add_rows(r.start, s[b * m:(b + 1) * m])
        scores = jnp.concatenate(pieces, axis=0)
        o = o + _dot(scores.astype(BF16), block_diag(v_bf))

        d_state = _dot_tn(v_bf, scaled(k_bf, until_end))
        decay_all = jnp.exp2(first_half + g_last)
        for i in range(2):
            hs = slice(i * hd, (i + 1) * hd)
            state_ref[2 * p + i] = st[i] * decay_all[:, hs] + d_state[hs, hs]
            oh = o[:, hs]
            oh = oh * lax.rsqrt(jnp.mean(oh * oh, axis=-1, keepdims=True) + NORM_EPS * hd)
            lanes = slice((2 * p + i) * hd, (2 * p + i + 1) * hd)
            y_ref[0, tok, lanes] = (oh * onorm_ref[:, lanes] * sg_ref[0, tok, lanes].astype(F32)).astype(BF16)


def _hgrn_scan(q, k, v, lf, sg, out_norm):
    b, s, d = q.shape
    c = HGRN_CHUNK
    a2, masks = _hgrn_tables(c)
    rows = HGRN_STEP_CHUNKS * c
    assert s % rows == 0
    tok = pl.BlockSpec((1, rows, d), lambda i, j: (i, j, 0))
    return pl.pallas_call(
        _hgrn_scan_kernel,
        grid=(b, s // rows),
        in_specs=[tok, tok, tok, tok, tok, _const_spec(a2.shape), _const_spec(masks.shape),
                  _const_spec((1, d))],
        out_specs=tok,
        out_shape=jax.ShapeDtypeStruct((b, s, d), BF16),
        scratch_shapes=[pltpu.VMEM((HGRN_HEADS, HGRN_HEAD_DIM, HGRN_HEAD_DIM), F32),
                        pltpu.VMEM((HGRN_STEP_CHUNKS, a2.shape[0], d), F32)],
        compiler_params=_params("arbitrary", "arbitrary"),
        name="hgrn_scan",
    )(q, k, v, lf, sg, a2, masks, out_norm)


BF16_ROWS = 16


def _perm_block(dilation):
    return max(ATTN_BLOCK, BF16_ROWS * dilation)


def _block_perm(dilation):
    bn = _perm_block(dilation)
    w = bn // dilation
    p = np.zeros((bn, bn), np.float32)
    for r in range(dilation):
        for i in range(w):
            p[r * w + i, i * dilation + r] = 1.0
    return p


def _to_group_order(rows, perm, dilation):
    bn = _perm_block(dilation)
    w = bn // dilation
    blocks = [_dot(perm, rows[q * bn:(q + 1) * bn]).astype(rows.dtype) for q in range(ROW_TILE // bn)]
    return jnp.concatenate([blk[r * w:(r + 1) * w] for r in range(dilation) for blk in blocks], axis=0)


def _to_natural_order(rows, unperm, dilation):
    bn = _perm_block(dilation)
    w = bn // dilation
    n = ROW_TILE // dilation
    out = []
    for q in range(ROW_TILE // bn):
        gathered = jnp.concatenate([rows[r * n + q * w:r * n + (q + 1) * w] for r in range(dilation)], axis=0)
        out.append(_dot(unperm, gathered))
    return jnp.concatenate(out, axis=0)


def _span_tiles(dilation):
    return max(1, ATTN_BLOCK * dilation // ROW_TILE)


def _grouped_view(a, dilation):
    t, n = a.shape
    k = _span_tiles(dilation)
    if k == 1:
        return a
    return a.reshape(t // (k * ROW_TILE), dilation, k, ROW_TILE // dilation, n)


def _grouped_spec(n, dilation):
    k = _span_tiles(dilation)
    if k == 1:
        return pl.BlockSpec((ROW_TILE, n), lambda i, *_: (i, 0))
    return pl.BlockSpec((None, dilation, None, ROW_TILE // dilation, n), lambda i, *_: (i // k, 0, i % k, 0, 0))


def _grouped_store(ref, dilation, val):
    if _span_tiles(dilation) == 1:
        ref[...] = val
    else:
        n = ROW_TILE // dilation
        for r in range(dilation):
            ref[r] = val[r * n:(r + 1) * n]


def _grouped_load(ref, dilation):
    if _span_tiles(dilation) == 1:
        return ref[...]
    return jnp.concatenate([ref[r] for r in range(dilation)], axis=0)


def _attn_qkv_kernel(dilation, x_ref, gain_ref, *rest):
    d = D_MODEL
    h = _rmsnorm_rows(x_ref[...], gain_ref[...]).astype(BF16)
    if dilation > 1:
        perm_ref, w_ref, o_ref = rest
        h = _to_group_order(h, perm_ref[...], dilation)
    else:
        w_ref, o_ref = rest
    res = _dot(h, w_ref[...])
    val = jnp.concatenate([(res[:, :d] * (ATTN_HEAD_DIM ** -0.5 * LOG2_E)).astype(BF16),
                           res[:, d:].astype(BF16)], axis=1)
    _grouped_store(o_ref, dilation, val)


def _attn_qkv(x2, gain, w, group, dilation):
    t, d = x2.shape
    n = 3 * d
    perm = [jnp.asarray(_block_perm(dilation), BF16)] if dilation > 1 else []
    w_spec = pl.BlockSpec((d, n), lambda i: (0, group), pipeline_mode=pl.Buffered(1))
    out = pl.pallas_call(
        functools.partial(_attn_qkv_kernel, dilation),
        grid=(t // ROW_TILE,),
        in_specs=[pl.BlockSpec((ROW_TILE, d), lambda i: (i, 0)), _const_spec((1, d))]
        + [_const_spec(p.shape) for p in perm] + [w_spec],
        out_specs=_grouped_spec(n, dilation),
        out_shape=jax.ShapeDtypeStruct(_grouped_view(jnp.empty((t, n), BF16), dilation).shape, BF16),
        compiler_params=_params("arbitrary"),
        name=f"attn_qkv_d{dilation}",
    )(x2, gain, *perm, w)
    return out.reshape(t, n)


def _attn_group_kernel(dilation, blocks_per_seq, q_ref, kp_ref, kc_ref, vp_ref, vc_ref, lm_ref, sel_ref, bias_ref,
                       o_ref, lse_ref):
    blk = ATTN_BLOCK
    nb = ATTN_STEP_BLOCKS
    m_lo, m_hi = lm_ref[0], lm_ref[1]
    low_q = lax.broadcasted_iota(jnp.int32, (blk, LANES), 1) < ATTN_HEAD_DIM

    for j in range(nb):
        rows = slice(j * blk, (j + 1) * blk)
        if dilation >= nb:
            prev_k, prev_v, prev_rows = kp_ref, vp_ref, rows
        elif j == 0:
            prev_k, prev_v, prev_rows = kp_ref, vp_ref, slice((nb - 1) * blk, nb * blk)
        else:
            prev_k, prev_v, prev_rows = kc_ref, vc_ref, slice((j - 1) * blk, j * blk)
        has_prev = (((pl.program_id(0) * nb + j) % blocks_per_seq) >= dilation).astype(jnp.int32)
        m_tile = jnp.zeros((blk, LANES), F32)
        l_tile = sel_ref[HEAD_PAIRS]

        for p in range(HEAD_PAIRS):
            sl = slice(p * LANES, (p + 1) * LANES)
            qp = q_ref[rows, sl]
            q_bd = jnp.concatenate([qp * m_lo[:blk], qp * m_hi[:blk]], axis=0)
            k_cat = jnp.concatenate([prev_k[prev_rows, sl], kc_ref[rows, sl]], axis=0)
            s = _dot_nt(q_bd, k_cat) + bias_ref[has_prev]
            s0, s1 = s[:blk], s[blk:]
            m0 = jnp.max(s0, axis=-1, keepdims=True)
            m1 = jnp.max(s1, axis=-1, keepdims=True)
            p_cat = jnp.concatenate([jnp.exp2(s0 - m0).astype(BF16), jnp.exp2(s1 - m1).astype(BF16)], axis=1)
            v_cat = jnp.concatenate([prev_v[prev_rows, sl], vc_ref[rows, sl]], axis=0)
            rhs = jnp.concatenate([jnp.concatenate([v_cat * m_lo, m_lo], axis=1),
                                   jnp.concatenate([v_cat * m_hi, m_hi], axis=1)], axis=0)
            res = _dot(p_cat, rhs)
            denom = res[:, LANES:]
            o_ref[rows, sl] = (res[:, :LANES] / denom).astype(o_ref.dtype)
            m_tile = m_tile + sel_ref[p] * jnp.where(low_q, m0, m1)
            l_tile = l_tile + sel_ref[p] * denom

        lse_ref[rows, :] = m_tile * LN_2 + jnp.log(l_tile)


def _attn_group(qkv, dilation, seq):
    t, n = qkv.shape
    d = D_MODEL
    nb = ATTN_STEP_BLOCKS
    blk = nb * ATTN_BLOCK
    assert dilation == 1 or dilation % nb == 0
    back = max(dilation // nb, 1)

    def spec(which, prev):
        if prev:
            return pl.BlockSpec((blk, d), lambda i: (jnp.maximum(i - back, 0), which))
        return pl.BlockSpec((blk, d), lambda i: (i, which))

    first = np.broadcast_to(np.arange(LANES) < ATTN_HEAD_DIM, (2 * ATTN_BLOCK, LANES))
    lane_masks = jnp.asarray(np.stack([first, ~first]).astype(np.float32), BF16)
    select = np.zeros((HEAD_PAIRS + 1, ATTN_BLOCK, LANES), np.float32)
    for p in range(HEAD_PAIRS):
        select[p, :, [p, ATTN_HEAD_DIM + p]] = 1.0
    select[HEAD_PAIRS] = 1.0 - select[:HEAD_PAIRS].sum(axis=0)
    lane_select = jnp.asarray(select)
    a = np.arange(2 * ATTN_BLOCK)[:, None] % ATTN_BLOCK
    c = np.arange(2 * ATTN_BLOCK)[None, :]
    own = (c >= ATTN_BLOCK) & (c - ATTN_BLOCK <= a)
    earlier = (c < ATTN_BLOCK) & (c >= a)
    bias = jnp.asarray(np.where(np.stack([own, own | earlier]), 0.0, -np.inf).astype(np.float32))

    return pl.pallas_call(
        functools.partial(_attn_group_kernel, dilation, seq // ATTN_BLOCK),
        grid=(t // blk,),
        in_specs=[spec(0, False), spec(1, True), spec(1, False), spec(2, True), spec(2, False),
                  _const_spec(lane_masks.shape), _const_spec(lane_select.shape), _const_spec(bias.shape)],
        out_specs=[pl.BlockSpec((blk, d), lambda i: (i, 0)), pl.BlockSpec((blk, LANES), lambda i: (i, 0))],
        out_shape=[jax.ShapeDtypeStruct((t, d), BF16), jax.ShapeDtypeStruct((t, LANES), F32)],
        compiler_params=_params("arbitrary"),
        name=f"attn_group_d{dilation}",
    )(qkv, qkv, qkv, qkv, qkv, lane_masks, lane_select, bias)


def _head_expand_matrix():
    e = np.zeros((LANES, D_MODEL), np.float32)
    for h in range(ATTN_HEADS):
        lane = (h // 2) + ATTN_HEAD_DIM * (h % 2)
        e[lane, h * ATTN_HEAD_DIM:(h + 1) * ATTN_HEAD_DIM] = 1.0
    return jnp.asarray(e, BF16)


def _mlp_tail(x1, gain_ref, w1_ref, w2_ref, final_gain_ref, o_ref):
    d = D_MODEL
    h = _rmsnorm_rows(x1, gain_ref[...]).astype(BF16)
    acc = x1
    for c in range(MLP_HIDDEN // d):
        a = jnp.maximum(_dot(h, w1_ref[:, c * d:(c + 1) * d]), 0.0)
        acc = acc + _dot((a * a).astype(BF16), w2_ref[c * d:(c + 1) * d, :])
    if final_gain_ref is not None:
        acc = _rmsnorm_rows(acc, final_gain_ref[...])
    o_ref[...] = acc


def _post_kernel(final, x_ref, y_ref, wo_ref, gain_ref, w1_ref, w2_ref, *rest):
    if final:
        final_gain_ref, o_ref = rest
    else:
        final_gain_ref, (o_ref,) = None, rest
    x1 = x_ref[...] + _dot(y_ref[...], wo_ref[...])
    _mlp_tail(x1, gain_ref, w1_ref, w2_ref, final_gain_ref, o_ref)


def _attn_post_kernel(final, x_ref, o0_ref, o1_ref, o2_ref, l0_ref, l1_ref, l2_ref, u1_ref, u2_ref, exp_ref,
                      wo_ref, gain_ref, w1_ref, w2_ref, *rest):
    if final:
        final_gain_ref, o_ref = rest
    else:
        final_gain_ref, (o_ref,) = None, rest
    dils = [dil for _, dil in DILATED_GROUPS]
    outs = [o0_ref[...]]
    lses = [l0_ref[...]]
    for o_g, l_g, u_ref, dil in zip((o1_ref, o2_ref), (l1_ref, l2_ref), (u1_ref, u2_ref), dils[1:]):
        outs.append(_to_natural_order(_grouped_load(o_g, dil), u_ref[...], dil).astype(BF16))
        both = _to_natural_order(jnp.concatenate(_split_bf16(_grouped_load(l_g, dil)), axis=1), u_ref[...], dil)
        lses.append(both[:, :LANES] + both[:, LANES:])
    m = jnp.maximum(jnp.maximum(lses[0], lses[1]), lses[2])
    es = [jnp.exp(l - m) for l in lses]
    inv = 1.0 / (es[0] + es[1] + es[2])
    y = None
    for e, o_nat in zip(es, outs):
        term = _dot((e * inv).astype(BF16), exp_ref[...]) * o_nat.astype(F32)
        y = term if y is None else y + term
    x1 = x_ref[...] + _dot(y.astype(BF16), wo_ref[...])
    _mlp_tail(x1, gain_ref, w1_ref, w2_ref, final_gain_ref, o_ref)


def _post(x2, mix, wo, gain, w1, w2, final_gain):
    t, d = x2.shape
    assert t % ROW_TILE == 0
    row = lambda i: (i, 0)
    final = final_gain is not None
    weights = [wo, gain, w1, w2] + ([final_gain] if final else [])
    weight_specs = [_const_spec(w.shape) for w in weights]
    if isinstance(mix, tuple):
        dils = [dil for _, dil in DILATED_GROUPS]
        unperms = [jnp.asarray(_block_perm(dil).T, BF16) for dil in dils[1:]]
        expand = _head_expand_matrix()
        body = functools.partial(_attn_post_kernel, final)
        views = [_grouped_view(a, dil) for a, dil in zip(mix, dils + dils)]
        args = [x2, *views, *unperms, expand] + weights
        in_specs = ([pl.BlockSpec((ROW_TILE, d), row)] + [_grouped_spec(d, dil) for dil in dils]
                    + [_grouped_spec(LANES, dil) for dil in dils]
                    + [_const_spec(u.shape) for u in unperms] + [_const_spec(expand.shape)] + weight_specs)
    else:
        body = functools.partial(_post_kernel, final)
        args = [x2, mix] + weights
        in_specs = [pl.BlockSpec((ROW_TILE, d), row)] * 2 + weight_specs
    return pl.pallas_call(
        body,
        grid=(t // ROW_TILE,),
        in_specs=in_specs,
        out_specs=pl.BlockSpec((ROW_TILE, d), row),
        out_shape=jax.ShapeDtypeStruct((t, d), F32),
        compiler_params=_params("arbitrary"),
        name="post_mlp",
    )(*args)


def kernel(x, norm_mix, norm_mlp, norm_final, hgrn_w_in, hgrn_lower_bound, hgrn_out_norm, hgrn_w_out,
           attn_w_qkv, attn_w_out, mlp_w1, mlp_w2):
    b, s, d = x.shape
    depth = norm_mix.shape[0]
    n_mixers = 2
    x2 = x.reshape(b * s, d)
    row = lambda v: v.reshape(1, -1).astype(F32)
    lbp = hgrn_lower_bound.astype(F32)

    for layer in range(depth):
        j = layer // n_mixers
        final_gain = row(norm_final) if layer == depth - 1 else None
        if layer % n_mixers == 0:
            q, k, v, lf, sg = _hgrn_pre(x2, row(norm_mix[layer]), hgrn_w_in[j].astype(BF16), lbp, j)
            to3 = lambda a: a.reshape(b, s, d)
            mix = _hgrn_scan(to3(q), to3(k), to3(v), to3(lf), to3(sg), row(hgrn_out_norm[j])).reshape(b * s, d)
            wo = hgrn_w_out[j]
        else:
            w_qkv = attn_w_qkv[j].astype(BF16)
            outs, lses = [], []
            for g, (window, dilation) in enumerate(DILATED_GROUPS):
                assert window // dilation == ATTN_BLOCK and s % (ATTN_BLOCK * dilation) == 0
                qkv = _attn_qkv(x2, row(norm_mix[layer]), w_qkv, g, dilation)
                o_g, lse_g = _attn_group(qkv, dilation, s)
                outs.append(o_g)
                lses.append(lse_g)
            mix = (*outs, *lses)
            wo = attn_w_out[j]
        x2 = _post(x2, mix, wo.astype(BF16), row(norm_mlp[layer]), mlp_w1[layer].astype(BF16),
                   mlp_w2[layer].astype(BF16), final_gain)
    return x2.reshape(b, s, d)
```

```python
import functools

import numpy as np
import jax
import jax.numpy as jnp
from jax import lax
from jax.experimental import pallas as pl
from jax.experimental.pallas import tpu as pltpu

F32 = jnp.float32
BF16 = jnp.bfloat16

LOG2_E = 1.4426950408889634
LN_2 = 0.6931471805599453

D_MODEL = 1024
NORM_EPS = 1e-6
MLP_HIDDEN = 4 * D_MODEL

HGRN_HEAD_DIM = 128
HGRN_HEADS = D_MODEL // HGRN_HEAD_DIM
HGRN_CHUNK = 128
HGRN_STEP_CHUNKS = 4
HGRN_HALF_ROWS = 16
HGRN_COMPACT_FROM = 16

ATTN_HEADS = 16
ATTN_HEAD_DIM = D_MODEL // ATTN_HEADS
DILATED_GROUPS = ((128, 1), (512, 4), (2048, 16))
N_GROUPS = len(DILATED_GROUPS)
ATTN_BLOCK = 128
ATTN_STEP_BLOCKS = 4
LANES = 128
HEAD_PAIRS = ATTN_HEADS // 2

VMEM_LIMIT_BYTES = 56 * 1024 * 1024

ROW_TILE = 512
PRE_COLS = 256


def _dot(a, b):
    return jnp.dot(a, b, preferred_element_type=F32)


def _dot_nt(a, b):
    return lax.dot_general(a, b, (((1,), (1,)), ((), ())), preferred_element_type=F32)


def _dot_tn(a, b):
    return lax.dot_general(a, b, (((0,), (0,)), ((), ())), preferred_element_type=F32)


def _split_bf16(x):
    hi = x.astype(BF16)
    return hi, (x - hi.astype(F32)).astype(BF16)


def _rmsnorm_rows(x, gain):
    return x * lax.rsqrt(jnp.mean(x * x, axis=-1, keepdims=True) + NORM_EPS) * gain


def _params(*semantics):
    return pltpu.CompilerParams(dimension_semantics=semantics, vmem_limit_bytes=VMEM_LIMIT_BYTES)


def _const_spec(shape):
    zeros = (0,) * len(shape)
    return pl.BlockSpec(shape, lambda *_: zeros, pipeline_mode=pl.Buffered(1))


def _layer_spec(stack, layer, col_block=0, cols=None):
    _, rows, n = stack.shape
    return pl.BlockSpec((None, rows, n if cols is None else cols), lambda *_: (layer, 0, col_block),
                        pipeline_mode=pl.Buffered(1))


def _hgrn_pre_kernel(layer_j, x_ref, gain_ref, w_ref, lbp_ref, q_ref, k_ref, v_ref, lf_ref, sg_ref):
    d = D_MODEL
    h = _rmsnorm_rows(x_ref[...], gain_ref[...]).astype(BF16)

    if layer_j > 0:
        p = lbp_ref[...]
        e = jnp.exp(p - jnp.max(p, axis=0, keepdims=True))
        sm = e / jnp.sum(e, axis=0, keepdims=True)
        lb = sm[1:2, :]
        for i in range(2, layer_j + 1):
            lb = lb + sm[i:i + 1, :]
        log_lb = jnp.log2(lb)
        log_1m_lb = jnp.log2(1.0 - lb)

    for j in range(d // PRE_COLS):
        out = slice(j * PRE_COLS, (j + 1) * PRE_COLS)
        proj = lambda blk: _dot(h, w_ref[:, blk * d + j * PRE_COLS:blk * d + (j + 1) * PRE_COLS])

        q = proj(0)
        q_ref[:, out] = (q / (1.0 + jnp.exp2(q * (-LOG2_E)))).astype(BF16)

        f = proj(1)
        f2 = f * LOG2_E
        ef = jnp.exp2(-jnp.abs(f2))
        inv = 1.0 / (1.0 + ef)
        log_sig = jnp.minimum(f2, 0.0) - jnp.log2(1.0 + ef)
        sig_neg = jnp.where(f >= 0.0, ef * inv, inv)
        if layer_j == 0:
            lf_ref[:, out] = log_sig
            k_ref[:, out] = sig_neg.astype(BF16)
        else:
            a = log_lb[:, out]
            b = log_1m_lb[:, out] + log_sig
            lf_ref[:, out] = jnp.maximum(a, b) + jnp.log2(1.0 + jnp.exp2(-jnp.abs(a - b)))
            k_ref[:, out] = ((1.0 - lb[:, out]) * sig_neg).astype(BF16)

        v_ref[:, out] = proj(2).astype(BF16)

        g = proj(3)
        sg_ref[:, out] = (g / (1.0 + jnp.exp2(g * (-LOG2_E)))).astype(BF16)


def _hgrn_pre(x2, gain, w_in, lbp, layer_j):
    t, d = x2.shape
    tm = min(ROW_TILE, t)
    row = lambda i: (i, 0)
    out_bf = jax.ShapeDtypeStruct((t, d), BF16)
    return pl.pallas_call(
        functools.partial(_hgrn_pre_kernel, layer_j),
        grid=(t // tm,),
        in_specs=[pl.BlockSpec((tm, d), row), _const_spec((1, d)), _layer_spec(w_in, layer_j),
                  _const_spec(lbp.shape)],
        out_specs=[pl.BlockSpec((tm, d), row)] * 5,
        out_shape=[out_bf, out_bf, out_bf, jax.ShapeDtypeStruct((t, d), F32), out_bf],
        compiler_params=_params("arbitrary"),
        name="hgrn_pre",
    )(x2, gain, w_in, lbp)


def _hgrn_levels(c):
    m, out = 1, []
    while m < c:
        out.append(m)
        m *= 2
    return out


def _hgrn_tables(c):
    levels = _hgrn_levels(c)
    idx = np.arange(c)
    x, j = idx[:, None], idx[None, :]
    blocks = []
    masks = [(x == j)]
    for m in levels:
        bnd = (x // (2 * m)) * (2 * m) + m - 1
        second = (x % (2 * m)) >= m
        blocks.append(np.where(second, (j > bnd) & (j <= x), (j > x) & (j <= bnd)))
        t, s = x, j
        mask = (t // (2 * m) == s // (2 * m)) & ((t % (2 * m)) >= m) & ((s % (2 * m)) < m)
        if m < HGRN_COMPACT_FROM:
            masks.append(mask)
        elif c // (2 * m) > 1:
            masks.append(mask[second[:, 0]])
    blocks.append(np.broadcast_to(idx[None, :] < c // 2, (HGRN_HALF_ROWS, c)))
    a = np.concatenate(blocks, axis=0).astype(np.float32)
    a2 = np.concatenate([a, a], axis=1)
    masks = np.concatenate(masks, axis=0).astype(np.float32)
    masks2 = np.concatenate([masks, masks], axis=1)
    return jnp.asarray(a2, BF16), jnp.asarray(masks2)


def _hgrn_scan_kernel(q_ref, k_ref, v_ref, lf_ref, sg_ref, a2_ref, mask_ref, onorm_ref, y_ref,
                      state_ref, g_ref):
    c = HGRN_CHUNK
    hd = HGRN_HEAD_DIM
    levels = _hgrn_levels(c)

    @pl.when(pl.program_id(1) == 0)
    def _():
        state_ref[...] = jnp.zeros_like(state_ref)

    for ci in range(HGRN_STEP_CHUNKS):
        lf_hi, lf_lo = _split_bf16(lf_ref[0, ci * c:(ci + 1) * c])
        g_ref[ci] = _dot(a2_ref[...], jnp.concatenate([lf_hi, lf_lo], axis=0))

    zeros = jnp.zeros((c, hd), BF16)
    zeros2 = jnp.zeros((c, 2 * hd), BF16)

    def block_diag(x):
        return jnp.concatenate([jnp.concatenate([x[:, :hd], zeros[:x.shape[0]]], axis=1),
                                jnp.concatenate([zeros[:x.shape[0]], x[:, hd:]], axis=1)], axis=0)

    def scaled(x_bf, log2_decay):
        return x_bf * jnp.exp2(log2_decay).astype(BF16)

    def decayed(x_bf, block, rows=slice(0, c)):
        return scaled(x_bf[rows], g_ref[ci, block * c + rows.start:block * c + rows.stop, sl])

    tile = 16
    half = c // 2
    for ci, p in [(ci, p) for ci in range(HGRN_STEP_CHUNKS) for p in range(HGRN_HEADS // 2)]:
        tok = slice(ci * c, (ci + 1) * c)
        sl = slice(2 * p * hd, 2 * (p + 1) * hd)
        q_bf = q_ref[0, tok, sl]
        k_bf = k_ref[0, tok, sl]
        v_bf = v_ref[0, tok, sl]
        st = [state_ref[2 * p], state_ref[2 * p + 1]]
        st_bd = block_diag(jnp.concatenate([st[0].astype(BF16), st[1].astype(BF16)], axis=1))

        top = (len(levels) - 1) * c
        g_lo = g_ref[ci, top:top + half, sl]
        g_hi = g_ref[ci, top + half:top + c, sl]
        first_half = g_ref[ci, len(levels) * c:len(levels) * c + 1, sl]
        g_last = g_hi[half - 1:half]
        since_start = jnp.concatenate([first_half - g_lo, first_half + g_hi], axis=0)
        until_end = jnp.concatenate([g_last + g_lo, g_last - g_hi], axis=0)

        o = _dot_nt(scaled(q_bf, since_start), st_bd)

        pieces = [None] * (c // tile)

        def add_rows(start, val):
            for i in range(val.shape[0] // tile):
                piece = val[i * tile:(i + 1) * tile]
                at = start // tile + i
                pieces[at] = piece if pieces[at] is None else pieces[at] + piece

        add_rows(0, mask_ref[0:c] * _dot_nt(q_bf, block_diag(k_bf)))
        mrow = c
        for li, m in enumerate(levels):
            blk = li
            if m < HGRN_COMPACT_FROM:
                s = _dot_nt(decayed(q_bf, blk), block_diag(decayed(k_bf, blk)))
                add_rows(0, mask_ref[mrow:mrow + c] * s)
                mrow += c
            else:
                n_blk = c // (2 * m)
                first = [slice(b * 2 * m, b * 2 * m + m) for b in range(n_blk)]
                second = [slice(b * 2 * m + m, (b + 1) * 2 * m) for b in range(n_blk)]
                lhs = jnp.concatenate([decayed(q_bf, blk, r) for r in second], axis=0)
                keys = jnp.concatenate([x for r in first for x in (decayed(k_bf, blk, r), zeros2[:m])], axis=0)
                s = _dot_nt(lhs, block_diag(keys))
                if n_blk > 1:
                    s = mask_ref[mrow:mrow + c // 2] * s
                    mrow += c // 2
                for b, r in enumerate(second):
                    add_rows(r.start, s[b * m:(b + 1) * m])
        scores = jnp.concatenate(pieces, axis=0)
        o = o + _dot(scores.astype(BF16), block_diag(v_bf))

        d_state = _dot_tn(v_bf, scaled(k_bf, until_end))
        decay_all = jnp.exp2(first_half + g_last)
        for i in range(2):
            hs = slice(i * hd, (i + 1) * hd)
            state_ref[2 * p + i] = st[i] * decay_all[:, hs] + d_state[hs, hs]
            oh = o[:, hs]
            oh = oh * lax.rsqrt(jnp.mean(oh * oh, axis=-1, keepdims=True) + NORM_EPS * hd)
            lanes = slice((2 * p + i) * hd, (2 * p + i + 1) * hd)
            y_ref[0, tok, lanes] = (oh * onorm_ref[:, lanes] * sg_ref[0, tok, lanes].astype(F32)).astype(BF16)


def _hgrn_scan(q, k, v, lf, sg, out_norm):
    b, s, d = q.shape
    c = HGRN_CHUNK
    a2, masks = _hgrn_tables(c)
    rows = HGRN_STEP_CHUNKS * c
    assert s % rows == 0
    tok = pl.BlockSpec((1, rows, d), lambda i, j: (i, j, 0))
    return pl.pallas_call(
        _hgrn_scan_kernel,
        grid=(b, s // rows),
        in_specs=[tok, tok, tok, tok, tok, _const_spec(a2.shape), _const_spec(masks.shape),
                  _const_spec((1, d))],
        out_specs=tok,
        out_shape=jax.ShapeDtypeStruct((b, s, d), BF16),
        scratch_shapes=[pltpu.VMEM((HGRN_HEADS, HGRN_HEAD_DIM, HGRN_HEAD_DIM), F32),
                        pltpu.VMEM((HGRN_STEP_CHUNKS, a2.shape[0], d), F32)],
        compiler_params=_params("arbitrary", "arbitrary"),
        name="hgrn_scan",
    )(q, k, v, lf, sg, a2, masks, out_norm)


BF16_ROWS = 16


def _perm_block(dilation):
    return max(ATTN_BLOCK, BF16_ROWS * dilation)


def _block_perm(dilation):
    bn = _perm_block(dilation)
    w = bn // dilation
    p = np.zeros((bn, bn), np.float32)
    for r in range(dilation):
        for i in range(w):
            p[r * w + i, i * dilation + r] = 1.0
    return p


def _to_group_order(rows, perm, dilation):
    bn = _perm_block(dilation)
    w = bn // dilation
    blocks = [_dot(perm, rows[q * bn:(q + 1) * bn]).astype(rows.dtype) for q in range(ROW_TILE // bn)]
    return jnp.concatenate([blk[r * w:(r + 1) * w] for r in range(dilation) for blk in blocks], axis=0)


def _to_natural_order(rows, unperm, dilation):
    bn = _perm_block(dilation)
    w = bn // dilation
    n = ROW_TILE // dilation
    out = []
    for q in range(ROW_TILE // bn):
        gathered = jnp.concatenate([rows[r * n + q * w:r * n + (q + 1) * w] for r in range(dilation)], axis=0)
        out.append(_dot(unperm, gathered))
    return jnp.concatenate(out, axis=0)


def _span_tiles(dilation):
    return max(1, ATTN_BLOCK * dilation // ROW_TILE)


def _grouped_view(a, dilation):
    t, n = a.shape
    k = _span_tiles(dilation)
    if k == 1:
        return a
    return a.reshape(t // (k * ROW_TILE), dilation, k, ROW_TILE // dilation, n)


def _grouped_spec(n, dilation):
    k = _span_tiles(dilation)
    if k == 1:
        return pl.BlockSpec((ROW_TILE, n), lambda i, *_: (i, 0))
    return pl.BlockSpec((None, dilation, None, ROW_TILE // dilation, n), lambda i, *_: (i // k, 0, i % k, 0, 0))


def _grouped_store(ref, dilation, val):
    if _span_tiles(dilation) == 1:
        ref[...] = val
    else:
        n = ROW_TILE // dilation
        for r in range(dilation):
            ref[r] = val[r * n:(r + 1) * n]


def _grouped_load(ref, dilation):
    if _span_tiles(dilation) == 1:
        return ref[...]
    return jnp.concatenate([ref[r] for r in range(dilation)], axis=0)


def _attn_qkv_kernel(dilation, x_ref, gain_ref, *rest):
    d = D_MODEL
    h = _rmsnorm_rows(x_ref[...], gain_ref[...]).astype(BF16)
    if dilation > 1:
        perm_ref, w_ref, o_ref = rest
        h = _to_group_order(h, perm_ref[...], dilation)
    else:
        w_ref, o_ref = rest
    res = _dot(h, w_ref[...])
    val = jnp.concatenate([(res[:, :d] * (ATTN_HEAD_DIM ** -0.5 * LOG2_E)).astype(BF16),
                           res[:, d:].astype(BF16)], axis=1)
    _grouped_store(o_ref, dilation, val)


def _attn_qkv(x2, gain, w, layer_j, group, dilation):
    t, d = x2.shape
    n = 3 * d
    perm = [jnp.asarray(_block_perm(dilation), BF16)] if dilation > 1 else []
    w_spec = _layer_spec(w, layer_j, group, n)
    out = pl.pallas_call(
        functools.partial(_attn_qkv_kernel, dilation),
        grid=(t // ROW_TILE,),
        in_specs=[pl.BlockSpec((ROW_TILE, d), lambda i: (i, 0)), _const_spec((1, d))]
        + [_const_spec(p.shape) for p in perm] + [w_spec],
        out_specs=_grouped_spec(n, dilation),
        out_shape=jax.ShapeDtypeStruct(_grouped_view(jnp.empty((t, n), BF16), dilation).shape, BF16),
        compiler_params=_params("arbitrary"),
        name=f"attn_qkv_d{dilation}",
    )(x2, gain, *perm, w)
    return out.reshape(t, n)


def _attn_group_kernel(dilation, blocks_per_seq, q_ref, kp_ref, kc_ref, vp_ref, vc_ref, lm_ref, sel_ref, bias_ref,
                       o_ref, lse_ref):
    blk = ATTN_BLOCK
    nb = ATTN_STEP_BLOCKS
    m_lo, m_hi = lm_ref[0], lm_ref[1]
    low_q = lax.broadcasted_iota(jnp.int32, (blk, LANES), 1) < ATTN_HEAD_DIM

    for j in range(nb):
        rows = slice(j * blk, (j + 1) * blk)
        if dilation >= nb:
            prev_k, prev_v, prev_rows = kp_ref, vp_ref, rows
        elif j == 0:
            prev_k, prev_v, prev_rows = kp_ref, vp_ref, slice((nb - 1) * blk, nb * blk)
        else:
            prev_k, prev_v, prev_rows = kc_ref, vc_ref, slice((j - 1) * blk, j * blk)
        has_prev = (((pl.program_id(0) * nb + j) % blocks_per_seq) >= dilation).astype(jnp.int32)
        m_tile = jnp.zeros((blk, LANES), F32)
        l_tile = sel_ref[HEAD_PAIRS]

        for p in range(HEAD_PAIRS):
            sl = slice(p * LANES, (p + 1) * LANES)
            qp = q_ref[rows, sl]
            q_bd = jnp.concatenate([qp * m_lo[:blk], qp * m_hi[:blk]], axis=0)
            k_cat = jnp.concatenate([prev_k[prev_rows, sl], kc_ref[rows, sl]], axis=0)
            s = _dot_nt(q_bd, k_cat) + bias_ref[has_prev]
            s0, s1 = s[:blk], s[blk:]
            m0 = jnp.max(s0, axis=-1, keepdims=True)
            m1 = jnp.max(s1, axis=-1, keepdims=True)
            p_cat = jnp.concatenate([jnp.exp2(s0 - m0).astype(BF16), jnp.exp2(s1 - m1).astype(BF16)], axis=1)
            v_cat = jnp.concatenate([prev_v[prev_rows, sl], vc_ref[rows, sl]], axis=0)
            rhs = jnp.concatenate([jnp.concatenate([v_cat * m_lo, m_lo], axis=1),
                                   jnp.concatenate([v_cat * m_hi, m_hi], axis=1)], axis=0)
            res = _dot(p_cat, rhs)
            denom = res[:, LANES:]
            o_ref[rows, sl] = (res[:, :LANES] / denom).astype(o_ref.dtype)
            m_tile = m_tile + sel_ref[p] * jnp.where(low_q, m0, m1)
            l_tile = l_tile + sel_ref[p] * denom

        lse_ref[rows, :] = m_tile * LN_2 + jnp.log(l_tile)


def _attn_group(qkv, dilation, seq):
    t, n = qkv.shape
    d = D_MODEL
    nb = ATTN_STEP_BLOCKS
    blk = nb * ATTN_BLOCK
    assert dilation == 1 or dilation % nb == 0
    back = max(dilation // nb, 1)

    def spec(which, prev):
        if prev:
            return pl.BlockSpec((blk, d), lambda i: (jnp.maximum(i - back, 0), which))
        return pl.BlockSpec((blk, d), lambda i: (i, which))

    first = np.broadcast_to(np.arange(LANES) < ATTN_HEAD_DIM, (2 * ATTN_BLOCK, LANES))
    lane_masks = jnp.asarray(np.stack([first, ~first]).astype(np.float32), BF16)
    select = np.zeros((HEAD_PAIRS + 1, ATTN_BLOCK, LANES), np.float32)
    for p in range(HEAD_PAIRS):
        select[p, :, [p, ATTN_HEAD_DIM + p]] = 1.0
    select[HEAD_PAIRS] = 1.0 - select[:HEAD_PAIRS].sum(axis=0)
    lane_select = jnp.asarray(select)
    a = np.arange(2 * ATTN_BLOCK)[:, None] % ATTN_BLOCK
    c = np.arange(2 * ATTN_BLOCK)[None, :]
    own = (c >= ATTN_BLOCK) & (c - ATTN_BLOCK <= a)
    earlier = (c < ATTN_BLOCK) & (c >= a)
    bias = jnp.asarray(np.where(np.stack([own, own | earlier]), 0.0, -np.inf).astype(np.float32))

    return pl.pallas_call(
        functools.partial(_attn_group_kernel, dilation, seq // ATTN_BLOCK),
        grid=(t // blk,),
        in_specs=[spec(0, False), spec(1, True), spec(1, False), spec(2, True), spec(2, False),
                  _const_spec(lane_masks.shape), _const_spec(lane_select.shape), _const_spec(bias.shape)],
        out_specs=[pl.BlockSpec((blk, d), lambda i: (i, 0)), pl.BlockSpec((blk, LANES), lambda i: (i, 0))],
        out_shape=[jax.ShapeDtypeStruct((t, d), BF16), jax.ShapeDtypeStruct((t, LANES), F32)],
        compiler_params=_params("arbitrary"),
        name=f"attn_group_d{dilation}",
    )(qkv, qkv, qkv, qkv, qkv, lane_masks, lane_select, bias)


def _head_expand_matrix():
    e = np.zeros((LANES, D_MODEL), np.float32)
    for h in range(ATTN_HEADS):
        lane = (h // 2) + ATTN_HEAD_DIM * (h % 2)
        e[lane, h * ATTN_HEAD_DIM:(h + 1) * ATTN_HEAD_DIM] = 1.0
    return jnp.asarray(e, BF16)


def _mlp_tail(x1, gain_ref, w1_ref, w2_ref, final_gain_ref, o_ref):
    d = D_MODEL
    h = _rmsnorm_rows(x1, gain_ref[...]).astype(BF16)
    acc = x1
    for c in range(MLP_HIDDEN // d):
        a = jnp.maximum(_dot(h, w1_ref[:, c * d:(c + 1) * d]), 0.0)
        acc = acc + _dot((a * a).astype(BF16), w2_ref[c * d:(c + 1) * d, :])
    if final_gain_ref is not None:
        acc = _rmsnorm_rows(acc, final_gain_ref[...])
    o_ref[...] = acc


def _post_kernel(final, x_ref, y_ref, wo_ref, gain_ref, w1_ref, w2_ref, *rest):
    if final:
        final_gain_ref, o_ref = rest
    else:
        final_gain_ref, (o_ref,) = None, rest
    x1 = x_ref[...] + _dot(y_ref[...], wo_ref[...])
    _mlp_tail(x1, gain_ref, w1_ref, w2_ref, final_gain_ref, o_ref)


def _attn_post_kernel(final, x_ref, o0_ref, o1_ref, o2_ref, l0_ref, l1_ref, l2_ref, u1_ref, u2_ref, exp_ref,
                      wo_ref, gain_ref, w1_ref, w2_ref, *rest):
    if final:
        final_gain_ref, o_ref = rest
    else:
        final_gain_ref, (o_ref,) = None, rest
    dils = [dil for _, dil in DILATED_GROUPS]
    outs = [o0_ref[...]]
    lses = [l0_ref[...]]
    for o_g, l_g, u_ref, dil in zip((o1_ref, o2_ref), (l1_ref, l2_ref), (u1_ref, u2_ref), dils[1:]):
        outs.append(_to_natural_order(_grouped_load(o_g, dil), u_ref[...], dil).astype(BF16))
        both = _to_natural_order(jnp.concatenate(_split_bf16(_grouped_load(l_g, dil)), axis=1), u_ref[...], dil)
        lses.append(both[:, :LANES] + both[:, LANES:])
    m = jnp.maximum(jnp.maximum(lses[0], lses[1]), lses[2])
    es = [jnp.exp(l - m) for l in lses]
    inv = 1.0 / (es[0] + es[1] + es[2])
    y = None
    for e, o_nat in zip(es, outs):
        term = _dot((e * inv).astype(BF16), exp_ref[...]) * o_nat.astype(F32)
        y = term if y is None else y + term
    x1 = x_ref[...] + _dot(y.astype(BF16), wo_ref[...])
    _mlp_tail(x1, gain_ref, w1_ref, w2_ref, final_gain_ref, o_ref)


def _post(x2, mix, wo, layer_j, gain, w1, w2, layer, final_gain):
    t, d = x2.shape
    assert t % ROW_TILE == 0
    row = lambda i: (i, 0)
    final = final_gain is not None
    weights = [wo, gain, w1, w2] + ([final_gain] if final else [])
    weight_specs = ([_layer_spec(wo, layer_j), _const_spec(gain.shape), _layer_spec(w1, layer), _layer_spec(w2, layer)]
                    + ([_const_spec(final_gain.shape)] if final else []))
    if isinstance(mix, tuple):
        dils = [dil for _, dil in DILATED_GROUPS]
        unperms = [jnp.asarray(_block_perm(dil).T, BF16) for dil in dils[1:]]
        expand = _head_expand_matrix()
        body = functools.partial(_attn_post_kernel, final)
        views = [_grouped_view(a, dil) for a, dil in zip(mix, dils + dils)]
        args = [x2, *views, *unperms, expand] + weights
        in_specs = ([pl.BlockSpec((ROW_TILE, d), row)] + [_grouped_spec(d, dil) for dil in dils]
                    + [_grouped_spec(LANES, dil) for dil in dils]
                    + [_const_spec(u.shape) for u in unperms] + [_const_spec(expand.shape)] + weight_specs)
    else:
        body = functools.partial(_post_kernel, final)
        args = [x2, mix] + weights
        in_specs = [pl.BlockSpec((ROW_TILE, d), row)] * 2 + weight_specs
    return pl.pallas_call(
        body,
        grid=(t // ROW_TILE,),
        in_specs=in_specs,
        out_specs=pl.BlockSpec((ROW_TILE, d), row),
        out_shape=jax.ShapeDtypeStruct((t, d), F32),
        compiler_params=_params("arbitrary"),
        name="post_mlp",
    )(*args)


def kernel(x, norm_mix, norm_mlp, norm_final, hgrn_w_in, hgrn_lower_bound, hgrn_out_norm, hgrn_w_out,
           attn_w_qkv, attn_w_out, mlp_w1, mlp_w2):
    b, s, d = x.shape
    depth = norm_mix.shape[0]
    n_mixers = 2
    x2 = x.reshape(b * s, d)
    row = lambda v: v.reshape(1, -1).astype(F32)
    lbp = hgrn_lower_bound.astype(F32)
    hgrn_w_in, hgrn_w_out, attn_w_qkv, attn_w_out, mlp_w1, mlp_w2 = (
        w.astype(BF16) for w in (hgrn_w_in, hgrn_w_out, attn_w_qkv, attn_w_out, mlp_w1, mlp_w2))

    for layer in range(depth):
        j = layer // n_mixers
        final_gain = row(norm_final) if layer == depth - 1 else None
        if layer % n_mixers == 0:
            q, k, v, lf, sg = _hgrn_pre(x2, row(norm_mix[layer]), hgrn_w_in, lbp, j)
            to3 = lambda a: a.reshape(b, s, d)
            mix = _hgrn_scan(to3(q), to3(k), to3(v), to3(lf), to3(sg), row(hgrn_out_norm[j])).reshape(b * s, d)
            wo = hgrn_w_out
        else:
            outs, lses = [], []
            for g, (window, dilation) in enumerate(DILATED_GROUPS):
                assert window // dilation == ATTN_BLOCK and s % (ATTN_BLOCK * dilation) == 0
                qkv = _attn_qkv(x2, row(norm_mix[layer]), attn_w_qkv, j, g, dilation)
                o_g, lse_g = _attn_group(qkv, dilation, s)
                outs.append(o_g)
                lses.append(lse_g)
            mix = (*outs, *lses)
            wo = attn_w_out
        x2 = _post(x2, mix, wo, j, row(norm_mlp[layer]), mlp_w1, mlp_w2, layer, final_gain)
    return x2.reshape(b, s, d)
```

```python
import functools

import numpy as np
import jax
import jax.numpy as jnp
from jax import lax
from jax.experimental import pallas as pl
from jax.experimental.pallas import tpu as pltpu

F32 = jnp.float32
BF16 = jnp.bfloat16

LOG2_E = 1.4426950408889634
LN_2 = 0.6931471805599453

D_MODEL = 1024
NORM_EPS = 1e-6
MLP_HIDDEN = 4 * D_MODEL

HGRN_HEAD_DIM = 128
HGRN_HEADS = D_MODEL // HGRN_HEAD_DIM
HGRN_CHUNK = 128
HGRN_STEP_CHUNKS = 4
HGRN_HALF_ROWS = 16
HGRN_COMPACT_FROM = 16

ATTN_HEADS = 16
ATTN_HEAD_DIM = D_MODEL // ATTN_HEADS
DILATED_GROUPS = ((128, 1), (512, 4), (2048, 16))
N_GROUPS = len(DILATED_GROUPS)
ATTN_BLOCK = 128
ATTN_STEP_BLOCKS = 8
LANES = 128
HEAD_PAIRS = ATTN_HEADS // 2

VMEM_LIMIT_BYTES = 56 * 1024 * 1024

ROW_TILE = 512
PRE_ROWS = 1024
PRE_COLS = 256


def _dot(a, b):
    return jnp.dot(a, b, preferred_element_type=F32)


def _dot_nt(a, b):
    return lax.dot_general(a, b, (((1,), (1,)), ((), ())), preferred_element_type=F32)


def _dot_tn(a, b):
    return lax.dot_general(a, b, (((0,), (0,)), ((), ())), preferred_element_type=F32)


def _split_bf16(x):
    hi = x.astype(BF16)
    return hi, (x - hi.astype(F32)).astype(BF16)


def _rmsnorm_rows(x, gain):
    return x * lax.rsqrt(jnp.mean(x * x, axis=-1, keepdims=True) + NORM_EPS) * gain


def _params(*semantics):
    return pltpu.CompilerParams(dimension_semantics=semantics, vmem_limit_bytes=VMEM_LIMIT_BYTES)


def _const_spec(shape):
    zeros = (0,) * len(shape)
    return pl.BlockSpec(shape, lambda *_: zeros, pipeline_mode=pl.Buffered(1))


def _layer_spec(stack, layer, col_block=0, cols=None):
    _, rows, n = stack.shape
    return pl.BlockSpec((None, rows, n if cols is None else cols), lambda *_: (layer, 0, col_block),
                        pipeline_mode=pl.Buffered(1))


def _hgrn_pre_kernel(layer_j, x_ref, gain_ref, w_ref, lbp_ref, q_ref, k_ref, v_ref, lf_ref, sg_ref):
    d = D_MODEL
    h = _rmsnorm_rows(x_ref[...], gain_ref[...]).astype(BF16)

    if layer_j > 0:
        p = lbp_ref[...]
        e = jnp.exp(p - jnp.max(p, axis=0, keepdims=True))
        sm = e / jnp.sum(e, axis=0, keepdims=True)
        lb = sm[1:2, :]
        for i in range(2, layer_j + 1):
            lb = lb + sm[i:i + 1, :]
        log_lb = jnp.log2(lb)
        log_1m_lb = jnp.log2(1.0 - lb)

    for j in range(d // PRE_COLS):
        out = slice(j * PRE_COLS, (j + 1) * PRE_COLS)
        proj = lambda blk: _dot(h, w_ref[:, blk * d + j * PRE_COLS:blk * d + (j + 1) * PRE_COLS])

        q = proj(0)
        q_ref[:, out] = (q / (1.0 + jnp.exp2(q * (-LOG2_E)))).astype(BF16)

        f = proj(1)
        f2 = f * LOG2_E
        ef = jnp.exp2(-jnp.abs(f2))
        inv = 1.0 / (1.0 + ef)
        log_sig = jnp.minimum(f2, 0.0) - jnp.log2(1.0 + ef)
        sig_neg = jnp.where(f >= 0.0, ef * inv, inv)
        if layer_j == 0:
            lf_ref[:, out] = log_sig
            k_ref[:, out] = sig_neg.astype(BF16)
        else:
            a = log_lb[:, out]
            b = log_1m_lb[:, out] + log_sig
            lf_ref[:, out] = jnp.maximum(a, b) + jnp.log2(1.0 + jnp.exp2(-jnp.abs(a - b)))
            k_ref[:, out] = ((1.0 - lb[:, out]) * sig_neg).astype(BF16)

        v_ref[:, out] = proj(2).astype(BF16)

        g = proj(3)
        sg_ref[:, out] = (g / (1.0 + jnp.exp2(g * (-LOG2_E)))).astype(BF16)


def _hgrn_pre(x2, gain, w_in, lbp, layer_j):
    t, d = x2.shape
    tm = min(PRE_ROWS, t)
    row = lambda i: (i, 0)
    out_bf = jax.ShapeDtypeStruct((t, d), BF16)
    return pl.pallas_call(
        functools.partial(_hgrn_pre_kernel, layer_j),
        grid=(t // tm,),
        in_specs=[pl.BlockSpec((tm, d), row), _const_spec((1, d)), _layer_spec(w_in, layer_j),
                  _const_spec(lbp.shape)],
        out_specs=[pl.BlockSpec((tm, d), row)] * 5,
        out_shape=[out_bf, out_bf, out_bf, jax.ShapeDtypeStruct((t, d), F32), out_bf],
        compiler_params=_params("arbitrary"),
        name="hgrn_pre",
    )(x2, gain, w_in, lbp)


def _hgrn_levels(c):
    m, out = 1, []
    while m < c:
        out.append(m)
        m *= 2
    return out


def _hgrn_tables(c):
    levels = _hgrn_levels(c)
    idx = np.arange(c)
    x, j = idx[:, None], idx[None, :]
    blocks = []
    masks = [(x == j)]
    for m in levels:
        bnd = (x // (2 * m)) * (2 * m) + m - 1
        second = (x % (2 * m)) >= m
        blocks.append(np.where(second, (j > bnd) & (j <= x), (j > x) & (j <= bnd)))
        t, s = x, j
        mask = (t // (2 * m) == s // (2 * m)) & ((t % (2 * m)) >= m) & ((s % (2 * m)) < m)
        if m < HGRN_COMPACT_FROM:
            masks.append(mask)
        elif c // (2 * m) > 1:
            masks.append(mask[second[:, 0]])
    blocks.append(np.broadcast_to(idx[None, :] < c // 2, (HGRN_HALF_ROWS, c)))
    a = np.concatenate(blocks, axis=0).astype(np.float32)
    a2 = np.concatenate([a, a], axis=1)
    masks = np.concatenate(masks, axis=0).astype(np.float32)
    masks2 = np.concatenate([masks, masks], axis=1)
    return jnp.asarray(a2, BF16), jnp.asarray(masks2)


def _hgrn_scan_kernel(q_ref, k_ref, v_ref, lf_ref, sg_ref, a2_ref, mask_ref, onorm_ref, y_ref,
                      state_ref, g_ref):
    c = HGRN_CHUNK
    hd = HGRN_HEAD_DIM
    levels = _hgrn_levels(c)

    @pl.when(pl.program_id(1) == 0)
    def _():
        state_ref[...] = jnp.zeros_like(state_ref)

    for ci in range(HGRN_STEP_CHUNKS):
        lf_hi, lf_lo = _split_bf16(lf_ref[0, ci * c:(ci + 1) * c])
        g_ref[ci] = _dot(a2_ref[...], jnp.concatenate([lf_hi, lf_lo], axis=0))

    zeros = jnp.zeros((c, hd), BF16)
    zeros2 = jnp.zeros((c, 2 * hd), BF16)

    def block_diag(x):
        return jnp.concatenate([jnp.concatenate([x[:, :hd], zeros[:x.shape[0]]], axis=1),
                                jnp.concatenate([zeros[:x.shape[0]], x[:, hd:]], axis=1)], axis=0)

    def scaled(x_bf, log2_decay):
        return x_bf * jnp.exp2(log2_decay).astype(BF16)

    def decayed(x_bf, block, rows=slice(0, c)):
        return scaled(x_bf[rows], g_ref[ci, block * c + rows.start:block * c + rows.stop, sl])

    tile = 16
    half = c // 2
    for ci, p in [(ci, p) for ci in range(HGRN_STEP_CHUNKS) for p in range(HGRN_HEADS // 2)]:
        tok = slice(ci * c, (ci + 1) * c)
        sl = slice(2 * p * hd, 2 * (p + 1) * hd)
        q_bf = q_ref[0, tok, sl]
        k_bf = k_ref[0, tok, sl]
        v_bf = v_ref[0, tok, sl]
        st = [state_ref[2 * p], state_ref[2 * p + 1]]
        st_bd = block_diag(jnp.concatenate([st[0].astype(BF16), st[1].astype(BF16)], axis=1))

        top = (len(levels) - 1) * c
        g_lo = g_ref[ci, top:top + half, sl]
        g_hi = g_ref[ci, top + half:top + c, sl]
        first_half = g_ref[ci, len(levels) * c:len(levels) * c + 1, sl]
        g_last = g_hi[half - 1:half]
        since_start = jnp.concatenate([first_half - g_lo, first_half + g_hi], axis=0)
        until_end = jnp.concatenate([g_last + g_lo, g_last - g_hi], axis=0)

        o = _dot_nt(scaled(q_bf, since_start), st_bd)

        pieces = [None] * (c // tile)

        def add_rows(start, val):
            for i in range(val.shape[0] // tile):
                piece = val[i * tile:(i + 1) * tile]
                at = start // tile + i
                pieces[at] = piece if pieces[at] is None else pieces[at] + piece

        add_rows(0, mask_ref[0:c] * _dot_nt(q_bf, block_diag(k_bf)))
        mrow = c
        for li, m in enumerate(levels):
            blk = li
            if m < HGRN_COMPACT_FROM:
                s = _dot_nt(decayed(q_bf, blk), block_diag(decayed(k_bf, blk)))
                add_rows(0, mask_ref[mrow:mrow + c] * s)
                mrow += c
            else:
                n_blk = c // (2 * m)
                first = [slice(b * 2 * m, b * 2 * m + m) for b in range(n_blk)]
                second = [slice(b * 2 * m + m, (b + 1) * 2 * m) for b in range(n_blk)]
                lhs = jnp.concatenate([decayed(q_bf, blk, r) for r in second], axis=0)
                keys = jnp.concatenate([x for r in first for x in (decayed(k_bf, blk, r), zeros2[:m])], axis=0)
                s = _dot_nt(lhs, block_diag(keys))
                if n_blk > 1:
                    s = mask_ref[mrow:mrow + c // 2] * s
                    mrow += c // 2
                for b, r in enumerate(second):
                    add_rows(r.start, s[b * m:(b + 1) * m])
        scores = jnp.concatenate(pieces, axis=0)
        o = o + _dot(scores.astype(BF16), block_diag(v_bf))

        d_state = _dot_tn(v_bf, scaled(k_bf, until_end))
        decay_all = jnp.exp2(first_half + g_last)
        for i in range(2):
            hs = slice(i * hd, (i + 1) * hd)
            state_ref[2 * p + i] = st[i] * decay_all[:, hs] + d_state[hs, hs]
            oh = o[:, hs]
            oh = oh * lax.rsqrt(jnp.mean(oh * oh, axis=-1, keepdims=True) + NORM_EPS * hd)
            lanes = slice((2 * p + i) * hd, (2 * p + i + 1) * hd)
            y_ref[0, tok, lanes] = (oh * onorm_ref[:, lanes] * sg_ref[0, tok, lanes].astype(F32)).astype(BF16)


def _hgrn_scan(q, k, v, lf, sg, out_norm):
    b, s, d = q.shape
    c = HGRN_CHUNK
    a2, masks = _hgrn_tables(c)
    rows = HGRN_STEP_CHUNKS * c
    assert s % rows == 0
    tok = pl.BlockSpec((1, rows, d), lambda i, j: (i, j, 0))
    return pl.pallas_call(
        _hgrn_scan_kernel,
        grid=(b, s // rows),
        in_specs=[tok, tok, tok, tok, tok, _const_spec(a2.shape), _const_spec(masks.shape),
                  _const_spec((1, d))],
        out_specs=tok,
        out_shape=jax.ShapeDtypeStruct((b, s, d), BF16),
        scratch_shapes=[pltpu.VMEM((HGRN_HEADS, HGRN_HEAD_DIM, HGRN_HEAD_DIM), F32),
                        pltpu.VMEM((HGRN_STEP_CHUNKS, a2.shape[0], d), F32)],
        compiler_params=_params("arbitrary", "arbitrary"),
        name="hgrn_scan",
    )(q, k, v, lf, sg, a2, masks, out_norm)


BF16_ROWS = 16


def _perm_block(dilation):
    return max(ATTN_BLOCK, BF16_ROWS * dilation)


def _block_perm(dilation):
    bn = _perm_block(dilation)
    w = bn // dilation
    p = np.zeros((bn, bn), np.float32)
    for r in range(dilation):
        for i in range(w):
            p[r * w + i, i * dilation + r] = 1.0
    return p


def _to_group_order(rows, perm, dilation):
    bn = _perm_block(dilation)
    w = bn // dilation
    blocks = [_dot(perm, rows[q * bn:(q + 1) * bn]).astype(rows.dtype) for q in range(ROW_TILE // bn)]
    return jnp.concatenate([blk[r * w:(r + 1) * w] for r in range(dilation) for blk in blocks], axis=0)


def _to_natural_order(rows, unperm, dilation):
    bn = _perm_block(dilation)
    w = bn // dilation
    n = ROW_TILE // dilation
    out = []
    for q in range(ROW_TILE // bn):
        gathered = jnp.concatenate([rows[r * n + q * w:r * n + (q + 1) * w] for r in range(dilation)], axis=0)
        out.append(_dot(unperm, gathered))
    return jnp.concatenate(out, axis=0)


def _span_tiles(dilation):
    return max(1, ATTN_BLOCK * dilation // ROW_TILE)


def _grouped_view(a, dilation):
    t, n = a.shape
    k = _span_tiles(dilation)
    if k == 1:
        return a
    return a.reshape(t // (k * ROW_TILE), dilation, k, ROW_TILE // dilation, n)


def _grouped_spec(n, dilation):
    k = _span_tiles(dilation)
    if k == 1:
        return pl.BlockSpec((ROW_TILE, n), lambda i, *_: (i, 0))
    return pl.BlockSpec((None, dilation, None, ROW_TILE // dilation, n), lambda i, *_: (i // k, 0, i % k, 0, 0))


def _grouped_store(ref, dilation, val):
    if _span_tiles(dilation) == 1:
        ref[...] = val
    else:
        n = ROW_TILE // dilation
        for r in range(dilation):
            ref[r] = val[r * n:(r + 1) * n]


def _grouped_load(ref, dilation):
    if _span_tiles(dilation) == 1:
        return ref[...]
    return jnp.concatenate([ref[r] for r in range(dilation)], axis=0)


def _attn_qkv_kernel(dilation, x_ref, gain_ref, *rest):
    d = D_MODEL
    h = _rmsnorm_rows(x_ref[...], gain_ref[...]).astype(BF16)
    if dilation > 1:
        perm_ref, w_ref, o_ref = rest
        h = _to_group_order(h, perm_ref[...], dilation)
    else:
        w_ref, o_ref = rest
    res = _dot(h, w_ref[...])
    val = jnp.concatenate([(res[:, :d] * (ATTN_HEAD_DIM ** -0.5 * LOG2_E)).astype(BF16),
                           res[:, d:].astype(BF16)], axis=1)
    _grouped_store(o_ref, dilation, val)


def _attn_qkv(x2, gain, w, layer_j, group, dilation):
    t, d = x2.shape
    n = 3 * d
    perm = [jnp.asarray(_block_perm(dilation), BF16)] if dilation > 1 else []
    w_spec = _layer_spec(w, layer_j, group, n)
    out = pl.pallas_call(
        functools.partial(_attn_qkv_kernel, dilation),
        grid=(t // ROW_TILE,),
        in_specs=[pl.BlockSpec((ROW_TILE, d), lambda i: (i, 0)), _const_spec((1, d))]
        + [_const_spec(p.shape) for p in perm] + [w_spec],
        out_specs=_grouped_spec(n, dilation),
        out_shape=jax.ShapeDtypeStruct(_grouped_view(jnp.empty((t, n), BF16), dilation).shape, BF16),
        compiler_params=_params("arbitrary"),
        name=f"attn_qkv_d{dilation}",
    )(x2, gain, *perm, w)
    return out.reshape(t, n)


def _attn_group_kernel(dilation, blocks_per_seq, q_ref, kp_ref, kc_ref, vp_ref, vc_ref, lm_ref, sel_ref, bias_ref,
                       o_ref, lse_ref):
    blk = ATTN_BLOCK
    nb = ATTN_STEP_BLOCKS
    m_lo, m_hi = lm_ref[0], lm_ref[1]
    low_q = lax.broadcasted_iota(jnp.int32, (blk, LANES), 1) < ATTN_HEAD_DIM

    for j in range(nb):
        rows = slice(j * blk, (j + 1) * blk)
        if dilation >= nb:
            prev_k, prev_v, jp = kp_ref, vp_ref, j
        elif j >= dilation:
            prev_k, prev_v, jp = kc_ref, vc_ref, j - dilation
        else:
            prev_k, prev_v, jp = kp_ref, vp_ref, j - dilation + nb
        prev_rows = slice(jp * blk, (jp + 1) * blk)
        has_prev = (((pl.program_id(0) * nb + j) % blocks_per_seq) >= dilation).astype(jnp.int32)
        m_tile = jnp.zeros((blk, LANES), F32)
        l_tile = sel_ref[HEAD_PAIRS]

        for p in range(HEAD_PAIRS):
            sl = slice(p * LANES, (p + 1) * LANES)
            qp = q_ref[rows, sl]
            q_bd = jnp.concatenate([qp * m_lo[:blk], qp * m_hi[:blk]], axis=0)
            k_cat = jnp.concatenate([prev_k[prev_rows, sl], kc_ref[rows, sl]], axis=0)
            s = _dot_nt(q_bd, k_cat) + bias_ref[has_prev]
            s0, s1 = s[:blk], s[blk:]
            m0 = jnp.max(s0, axis=-1, keepdims=True)
            m1 = jnp.max(s1, axis=-1, keepdims=True)
            p_cat = jnp.concatenate([jnp.exp2(s0 - m0).astype(BF16), jnp.exp2(s1 - m1).astype(BF16)], axis=1)
            v_cat = jnp.concatenate([prev_v[prev_rows, sl], vc_ref[rows, sl]], axis=0)
            rhs = jnp.concatenate([jnp.concatenate([v_cat * m_lo, m_lo], axis=1),
                                   jnp.concatenate([v_cat * m_hi, m_hi], axis=1)], axis=0)
            res = _dot(p_cat, rhs)
            denom = res[:, LANES:]
            o_ref[rows, sl] = (res[:, :LANES] / denom).astype(o_ref.dtype)
            m_tile = m_tile + sel_ref[p] * jnp.where(low_q, m0, m1)
            l_tile = l_tile + sel_ref[p] * denom

        lse_ref[rows, :] = m_tile * LN_2 + jnp.log(l_tile)


def _attn_group(qkv, dilation, seq):
    t, n = qkv.shape
    d = D_MODEL
    nb = ATTN_STEP_BLOCKS
    blk = nb * ATTN_BLOCK
    assert dilation % nb == 0 or nb % dilation == 0
    back = max(dilation // nb, 1)

    def spec(which, prev):
        if prev:
            return pl.BlockSpec((blk, d), lambda i: (jnp.maximum(i - back, 0), which))
        return pl.BlockSpec((blk, d), lambda i: (i, which))

    first = np.broadcast_to(np.arange(LANES) < ATTN_HEAD_DIM, (2 * ATTN_BLOCK, LANES))
    lane_masks = jnp.asarray(np.stack([first, ~first]).astype(np.float32), BF16)
    select = np.zeros((HEAD_PAIRS + 1, ATTN_BLOCK, LANES), np.float32)
    for p in range(HEAD_PAIRS):
        select[p, :, [p, ATTN_HEAD_DIM + p]] = 1.0
    select[HEAD_PAIRS] = 1.0 - select[:HEAD_PAIRS].sum(axis=0)
    lane_select = jnp.asarray(select)
    a = np.arange(2 * ATTN_BLOCK)[:, None] % ATTN_BLOCK
    c = np.arange(2 * ATTN_BLOCK)[None, :]
    own = (c >= ATTN_BLOCK) & (c - ATTN_BLOCK <= a)
    earlier = (c < ATTN_BLOCK) & (c >= a)
    bias = jnp.asarray(np.where(np.stack([own, own | earlier]), 0.0, -np.inf).astype(np.float32))

    return pl.pallas_call(
        functools.partial(_attn_group_kernel, dilation, seq // ATTN_BLOCK),
        grid=(t // blk,),
        in_specs=[spec(0, False), spec(1, True), spec(1, False), spec(2, True), spec(2, False),
                  _const_spec(lane_masks.shape), _const_spec(lane_select.shape), _const_spec(bias.shape)],
        out_specs=[pl.BlockSpec((blk, d), lambda i: (i, 0)), pl.BlockSpec((blk, LANES), lambda i: (i, 0))],
        out_shape=[jax.ShapeDtypeStruct((t, d), BF16), jax.ShapeDtypeStruct((t, LANES), F32)],
        compiler_params=_params("arbitrary"),
        name=f"attn_group_d{dilation}",
    )(qkv, qkv, qkv, qkv, qkv, lane_masks, lane_select, bias)


def _head_expand_matrix():
    e = np.zeros((LANES, D_MODEL), np.float32)
    for h in range(ATTN_HEADS):
        lane = (h // 2) + ATTN_HEAD_DIM * (h % 2)
        e[lane, h * ATTN_HEAD_DIM:(h + 1) * ATTN_HEAD_DIM] = 1.0
    return jnp.asarray(e, BF16)


def _mlp_tail(x1, gain_ref, w1_ref, w2_ref, final_gain_ref, o_ref):
    d = D_MODEL
    h = _rmsnorm_rows(x1, gain_ref[...]).astype(BF16)
    acc = x1
    for c in range(MLP_HIDDEN // d):
        a = jnp.maximum(_dot(h, w1_ref[:, c * d:(c + 1) * d]), 0.0)
        acc = acc + _dot((a * a).astype(BF16), w2_ref[c * d:(c + 1) * d, :])
    if final_gain_ref is not None:
        acc = _rmsnorm_rows(acc, final_gain_ref[...])
    o_ref[...] = acc


def _post_kernel(final, x_ref, y_ref, wo_ref, gain_ref, w1_ref, w2_ref, *rest):
    if final:
        final_gain_ref, o_ref = rest
    else:
        final_gain_ref, (o_ref,) = None, rest
    x1 = x_ref[...] + _dot(y_ref[...], wo_ref[...])
    _mlp_tail(x1, gain_ref, w1_ref, w2_ref, final_gain_ref, o_ref)


def _attn_post_kernel(final, x_ref, o0_ref, o1_ref, o2_ref, l0_ref, l1_ref, l2_ref, u1_ref, u2_ref, exp_ref,
                      wo_ref, gain_ref, w1_ref, w2_ref, *rest):
    if final:
        final_gain_ref, o_ref = rest
    else:
        final_gain_ref, (o_ref,) = None, rest
    dils = [dil for _, dil in DILATED_GROUPS]
    outs = [o0_ref[...]]
    lses = [l0_ref[...]]
    for o_g, l_g, u_ref, dil in zip((o1_ref, o2_ref), (l1_ref, l2_ref), (u1_ref, u2_ref), dils[1:]):
        outs.append(_to_natural_order(_grouped_load(o_g, dil), u_ref[...], dil).astype(BF16))
        both = _to_natural_order(jnp.concatenate(_split_bf16(_grouped_load(l_g, dil)), axis=1), u_ref[...], dil)
        lses.append(both[:, :LANES] + both[:, LANES:])
    m = jnp.maximum(jnp.maximum(lses[0], lses[1]), lses[2])
    es = [jnp.exp(l - m) for l in lses]
    inv = 1.0 / (es[0] + es[1] + es[2])
    y = None
    for e, o_nat in zip(es, outs):
        term = _dot((e * inv).astype(BF16), exp_ref[...]) * o_nat.astype(F32)
        y = term if y is None else y + term
    x1 = x_ref[...] + _dot(y.astype(BF16), wo_ref[...])
    _mlp_tail(x1, gain_ref, w1_ref, w2_ref, final_gain_ref, o_ref)


def _post(x2, mix, wo, layer_j, gain, w1, w2, layer, final_gain):
    t, d = x2.shape
    assert t % ROW_TILE == 0
    row = lambda i: (i, 0)
    final = final_gain is not None
    weights = [wo, gain, w1, w2] + ([final_gain] if final else [])
    weight_specs = ([_layer_spec(wo, layer_j), _const_spec(gain.shape), _layer_spec(w1, layer), _layer_spec(w2, layer)]
                    + ([_const_spec(final_gain.shape)] if final else []))
    if isinstance(mix, tuple):
        dils = [dil for _, dil in DILATED_GROUPS]
        unperms = [jnp.asarray(_block_perm(dil).T, BF16) for dil in dils[1:]]
        expand = _head_expand_matrix()
        body = functools.partial(_attn_post_kernel, final)
        views = [_grouped_view(a, dil) for a, dil in zip(mix, dils + dils)]
        args = [x2, *views, *unperms, expand] + weights
        in_specs = ([pl.BlockSpec((ROW_TILE, d), row)] + [_grouped_spec(d, dil) for dil in dils]
                    + [_grouped_spec(LANES, dil) for dil in dils]
                    + [_const_spec(u.shape) for u in unperms] + [_const_spec(expand.shape)] + weight_specs)
    else:
        body = functools.partial(_post_kernel, final)
        args = [x2, mix] + weights
        in_specs = [pl.BlockSpec((ROW_TILE, d), row)] * 2 + weight_specs
    return pl.pallas_call(
        body,
        grid=(t // ROW_TILE,),
        in_specs=in_specs,
        out_specs=pl.BlockSpec((ROW_TILE, d), row),
        out_shape=jax.ShapeDtypeStruct((t, d), F32),
        compiler_params=_params("arbitrary"),
        name="post_mlp",
    )(*args)


def kernel(x, norm_mix, norm_mlp, norm_final, hgrn_w_in, hgrn_lower_bound, hgrn_out_norm, hgrn_w_out,
           attn_w_qkv, attn_w_out, mlp_w1, mlp_w2):
    b, s, d = x.shape
    depth = norm_mix.shape[0]
    n_mixers = 2
    x2 = x.reshape(b * s, d)
    row = lambda v: v.reshape(1, -1).astype(F32)
    lbp = hgrn_lower_bound.astype(F32)
    hgrn_w_in, hgrn_w_out, attn_w_qkv, attn_w_out, mlp_w1, mlp_w2 = (
        w.astype(BF16) for w in (hgrn_w_in, hgrn_w_out, attn_w_qkv, attn_w_out, mlp_w1, mlp_w2))

    for layer in range(depth):
        j = layer // n_mixers
        final_gain = row(norm_final) if layer == depth - 1 else None
        if layer % n_mixers == 0:
            q, k, v, lf, sg = _hgrn_pre(x2, row(norm_mix[layer]), hgrn_w_in, lbp, j)
            to3 = lambda a: a.reshape(b, s, d)
            mix = _hgrn_scan(to3(q), to3(k), to3(v), to3(lf), to3(sg), row(hgrn_out_norm[j])).reshape(b * s, d)
            wo = hgrn_w_out
        else:
            outs, lses = [], []
            for g, (window, dilation) in enumerate(DILATED_GROUPS):
                assert window // dilation == ATTN_BLOCK and s % (ATTN_BLOCK * dilation) == 0
                qkv = _attn_qkv(x2, row(norm_mix[layer]), attn_w_qkv, j, g, dilation)
                o_g, lse_g = _attn_group(qkv, dilation, s)
                outs.append(o_g)
                lses.append(lse_g)
            mix = (*outs, *lses)
            wo = attn_w_out
        x2 = _post(x2, mix, wo, j, row(norm_mlp[layer]), mlp_w1, mlp_w2, layer, final_gain)
    return x2.reshape(b, s, d)
```

```python
import functools

import numpy as np
import jax
import jax.numpy as jnp
from jax import lax
from jax.experimental import pallas as pl
from jax.experimental.pallas import tpu as pltpu

F32 = jnp.float32
BF16 = jnp.bfloat16

LOG2_E = 1.4426950408889634
LN_2 = 0.6931471805599453

D_MODEL = 1024
NORM_EPS = 1e-6
MLP_HIDDEN = 4 * D_MODEL

HGRN_HEAD_DIM = 128
HGRN_HEADS = D_MODEL // HGRN_HEAD_DIM
HGRN_CHUNK = 128
HGRN_STEP_CHUNKS = 4

ATTN_HEADS = 16
ATTN_HEAD_DIM = D_MODEL // ATTN_HEADS
DILATED_GROUPS = ((128, 1), (512, 4), (2048, 16))
N_GROUPS = len(DILATED_GROUPS)
ATTN_BLOCK = 128
ATTN_STEP_BLOCKS = 8
LANES = 128
BF16_ROWS = 16
HGRN_HALF_ROWS = BF16_ROWS
HGRN_COMPACT_FROM = BF16_ROWS
HEAD_PAIRS = ATTN_HEADS // 2

VMEM_LIMIT_BYTES = 56 * 1024 * 1024

ROW_TILE = 512
PRE_ROWS = 1024
PRE_COLS = 256


def _dot(a, b):
    return jnp.dot(a, b, preferred_element_type=F32)


def _dot_nt(a, b):
    return lax.dot_general(a, b, (((1,), (1,)), ((), ())), preferred_element_type=F32)


def _dot_tn(a, b):
    return lax.dot_general(a, b, (((0,), (0,)), ((), ())), preferred_element_type=F32)


def _split_bf16(x):
    hi = x.astype(BF16)
    return hi, (x - hi.astype(F32)).astype(BF16)


def _rmsnorm_rows(x, gain):
    return x * lax.rsqrt(jnp.mean(x * x, axis=-1, keepdims=True) + NORM_EPS) * gain


def _params(*semantics):
    return pltpu.CompilerParams(dimension_semantics=semantics, vmem_limit_bytes=VMEM_LIMIT_BYTES)


def _const_spec(shape):
    zeros = (0,) * len(shape)
    return pl.BlockSpec(shape, lambda *_: zeros, pipeline_mode=pl.Buffered(1))


def _layer_spec(stack, layer, col_block=0, cols=None):
    _, rows, n = stack.shape
    return pl.BlockSpec((None, rows, n if cols is None else cols), lambda *_: (layer, 0, col_block),
                        pipeline_mode=pl.Buffered(1))


def _hgrn_pre_kernel(layer_j, x_ref, gain_ref, w_ref, lbp_ref, q_ref, k_ref, v_ref, lf_ref, sg_ref):
    d = D_MODEL
    h = _rmsnorm_rows(x_ref[...], gain_ref[...]).astype(BF16)

    if layer_j > 0:
        p = lbp_ref[...]
        e = jnp.exp(p - jnp.max(p, axis=0, keepdims=True))
        sm = e / jnp.sum(e, axis=0, keepdims=True)
        lb = sm[1:2, :]
        for i in range(2, layer_j + 1):
            lb = lb + sm[i:i + 1, :]
        log_lb = jnp.log2(lb)
        log_1m_lb = jnp.log2(1.0 - lb)

    for j in range(d // PRE_COLS):
        out = slice(j * PRE_COLS, (j + 1) * PRE_COLS)
        proj = lambda blk: _dot(h, w_ref[:, blk * d + j * PRE_COLS:blk * d + (j + 1) * PRE_COLS])

        q = proj(0)
        q_ref[:, out] = (q / (1.0 + jnp.exp2(q * (-LOG2_E)))).astype(BF16)

        f = proj(1)
        f2 = f * LOG2_E
        ef = jnp.exp2(-jnp.abs(f2))
        inv = 1.0 / (1.0 + ef)
        log_sig = jnp.minimum(f2, 0.0) - jnp.log2(1.0 + ef)
        sig_neg = jnp.where(f >= 0.0, ef * inv, inv)
        if layer_j == 0:
            lf_ref[:, out] = log_sig
            k_ref[:, out] = sig_neg.astype(BF16)
        else:
            a = log_lb[:, out]
            b = log_1m_lb[:, out] + log_sig
            lf_ref[:, out] = jnp.maximum(a, b) + jnp.log2(1.0 + jnp.exp2(-jnp.abs(a - b)))
            k_ref[:, out] = ((1.0 - lb[:, out]) * sig_neg).astype(BF16)

        v_ref[:, out] = proj(2).astype(BF16)

        g = proj(3)
        sg_ref[:, out] = (g / (1.0 + jnp.exp2(g * (-LOG2_E)))).astype(BF16)


def _hgrn_pre(x2, gain, w_in, lbp, layer_j):
    t, d = x2.shape
    tm = min(PRE_ROWS, t)
    row = lambda i: (i, 0)
    out_bf = jax.ShapeDtypeStruct((t, d), BF16)
    return pl.pallas_call(
        functools.partial(_hgrn_pre_kernel, layer_j),
        grid=(t // tm,),
        in_specs=[pl.BlockSpec((tm, d), row), _const_spec((1, d)), _layer_spec(w_in, layer_j),
                  _const_spec(lbp.shape)],
        out_specs=[pl.BlockSpec((tm, d), row)] * 5,
        out_shape=[out_bf, out_bf, out_bf, jax.ShapeDtypeStruct((t, d), F32), out_bf],
        compiler_params=_params("arbitrary"),
        name="hgrn_pre",
    )(x2, gain, w_in, lbp)


def _hgrn_levels(c):
    m, out = 1, []
    while m < c:
        out.append(m)
        m *= 2
    return out


def _hgrn_tables(c):
    levels = _hgrn_levels(c)
    idx = np.arange(c)
    x, j = idx[:, None], idx[None, :]
    blocks = []
    masks = [(x == j)]
    for m in levels:
        bnd = (x // (2 * m)) * (2 * m) + m - 1
        second = (x % (2 * m)) >= m
        blocks.append(np.where(second, (j > bnd) & (j <= x), (j > x) & (j <= bnd)))
        t, s = x, j
        mask = (t // (2 * m) == s // (2 * m)) & ((t % (2 * m)) >= m) & ((s % (2 * m)) < m)
        if m < HGRN_COMPACT_FROM:
            masks.append(mask)
        elif c // (2 * m) > 1:
            masks.append(mask[second[:, 0]])
    blocks.append(np.broadcast_to(idx[None, :] < c // 2, (HGRN_HALF_ROWS, c)))
    a = np.concatenate(blocks, axis=0).astype(np.float32)
    a2 = np.concatenate([a, a], axis=1)
    masks = np.concatenate(masks, axis=0).astype(np.float32)
    masks2 = np.concatenate([masks, masks], axis=1)
    return jnp.asarray(a2, BF16), jnp.asarray(masks2)


def _hgrn_scan_kernel(q_ref, k_ref, v_ref, lf_ref, sg_ref, a2_ref, mask_ref, onorm_ref, y_ref,
                      state_ref, g_ref):
    c = HGRN_CHUNK
    hd = HGRN_HEAD_DIM
    levels = _hgrn_levels(c)

    @pl.when(pl.program_id(1) == 0)
    def _():
        state_ref[...] = jnp.zeros_like(state_ref)

    for ci in range(HGRN_STEP_CHUNKS):
        lf_hi, lf_lo = _split_bf16(lf_ref[0, ci * c:(ci + 1) * c])
        g_ref[ci] = _dot(a2_ref[...], jnp.concatenate([lf_hi, lf_lo], axis=0))

    zeros = jnp.zeros((c, hd), BF16)
    zeros2 = jnp.zeros((c, 2 * hd), BF16)

    def block_diag(x):
        return jnp.concatenate([jnp.concatenate([x[:, :hd], zeros[:x.shape[0]]], axis=1),
                                jnp.concatenate([zeros[:x.shape[0]], x[:, hd:]], axis=1)], axis=0)

    def scaled(x_bf, log2_decay):
        return x_bf * jnp.exp2(log2_decay).astype(BF16)

    def decayed(x_bf, block, rows=slice(0, c)):
        return scaled(x_bf[rows], g_ref[ci, block * c + rows.start:block * c + rows.stop, sl])

    tile = BF16_ROWS
    half = c // 2
    for ci, p in [(ci, p) for ci in range(HGRN_STEP_CHUNKS) for p in range(HGRN_HEADS // 2)]:
        tok = slice(ci * c, (ci + 1) * c)
        sl = slice(2 * p * hd, 2 * (p + 1) * hd)
        q_bf = q_ref[0, tok, sl]
        k_bf = k_ref[0, tok, sl]
        v_bf = v_ref[0, tok, sl]
        st = [state_ref[2 * p], state_ref[2 * p + 1]]
        st_bd = block_diag(jnp.concatenate([st[0].astype(BF16), st[1].astype(BF16)], axis=1))

        top = (len(levels) - 1) * c
        g_lo = g_ref[ci, top:top + half, sl]
        g_hi = g_ref[ci, top + half:top + c, sl]
        first_half = g_ref[ci, len(levels) * c:len(levels) * c + 1, sl]
        g_last = g_hi[half - 1:half]
        since_start = jnp.concatenate([first_half - g_lo, first_half + g_hi], axis=0)
        until_end = jnp.concatenate([g_last + g_lo, g_last - g_hi], axis=0)

        o = _dot_nt(scaled(q_bf, since_start), st_bd)

        pieces = [None] * (c // tile)

        def add_rows(start, val):
            for i in range(val.shape[0] // tile):
                piece = val[i * tile:(i + 1) * tile]
                at = start // tile + i
                pieces[at] = piece if pieces[at] is None else pieces[at] + piece

        add_rows(0, mask_ref[0:c] * _dot_nt(q_bf, block_diag(k_bf)))
        mrow = c
        for li, m in enumerate(levels):
            blk = li
            if m < HGRN_COMPACT_FROM:
                s = _dot_nt(decayed(q_bf, blk), block_diag(decayed(k_bf, blk)))
                add_rows(0, mask_ref[mrow:mrow + c] * s)
                mrow += c
            else:
                n_blk = c // (2 * m)
                first = [slice(b * 2 * m, b * 2 * m + m) for b in range(n_blk)]
                second = [slice(b * 2 * m + m, (b + 1) * 2 * m) for b in range(n_blk)]
                lhs = jnp.concatenate([decayed(q_bf, blk, r) for r in second], axis=0)
                keys = jnp.concatenate([x for r in first for x in (decayed(k_bf, blk, r), zeros2[:m])], axis=0)
                s = _dot_nt(lhs, block_diag(keys))
                if n_blk > 1:
                    s = mask_ref[mrow:mrow + c // 2] * s
                    mrow += c // 2
                for b, r in enumerate(second):
                    add_rows(r.start, s[b * m:(b + 1) * m])
        scores = jnp.concatenate(pieces, axis=0)
        o = o + _dot(scores.astype(BF16), block_diag(v_bf))

        d_state = _dot_tn(v_bf, scaled(k_bf, until_end))
        decay_all = jnp.exp2(first_half + g_last)
        for i in range(2):
            hs = slice(i * hd, (i + 1) * hd)
            state_ref[2 * p + i] = st[i] * decay_all[:, hs] + d_state[hs, hs]
            oh = o[:, hs]
            oh = oh * lax.rsqrt(jnp.mean(oh * oh, axis=-1, keepdims=True) + NORM_EPS * hd)
            lanes = slice((2 * p + i) * hd, (2 * p + i + 1) * hd)
            y_ref[0, tok, lanes] = (oh * onorm_ref[:, lanes] * sg_ref[0, tok, lanes].astype(F32)).astype(BF16)


def _hgrn_scan(q, k, v, lf, sg, out_norm):
    b, s, d = q.shape
    c = HGRN_CHUNK
    a2, masks = _hgrn_tables(c)
    rows = HGRN_STEP_CHUNKS * c
    assert s % rows == 0
    tok = pl.BlockSpec((1, rows, d), lambda i, j: (i, j, 0))
    return pl.pallas_call(
        _hgrn_scan_kernel,
        grid=(b, s // rows),
        in_specs=[tok, tok, tok, tok, tok, _const_spec(a2.shape), _const_spec(masks.shape),
                  _const_spec((1, d))],
        out_specs=tok,
        out_shape=jax.ShapeDtypeStruct((b, s, d), BF16),
        scratch_shapes=[pltpu.VMEM((HGRN_HEADS, HGRN_HEAD_DIM, HGRN_HEAD_DIM), F32),
                        pltpu.VMEM((HGRN_STEP_CHUNKS, a2.shape[0], d), F32)],
        compiler_params=_params("arbitrary", "arbitrary"),
        name="hgrn_scan",
    )(q, k, v, lf, sg, a2, masks, out_norm)


def _perm_block(dilation):
    return max(ATTN_BLOCK, BF16_ROWS * dilation)


def _block_perm(dilation):
    bn = _perm_block(dilation)
    w = bn // dilation
    p = np.zeros((bn, bn), np.float32)
    for r in range(dilation):
        for i in range(w):
            p[r * w + i, i * dilation + r] = 1.0
    return p


def _to_group_order(rows, perm, dilation):
    bn = _perm_block(dilation)
    w = bn // dilation
    blocks = [_dot(perm, rows[q * bn:(q + 1) * bn]).astype(rows.dtype) for q in range(ROW_TILE // bn)]
    return jnp.concatenate([blk[r * w:(r + 1) * w] for r in range(dilation) for blk in blocks], axis=0)


def _to_natural_order(rows, unperm, dilation):
    bn = _perm_block(dilation)
    w = bn // dilation
    n = ROW_TILE // dilation
    out = []
    for q in range(ROW_TILE // bn):
        gathered = jnp.concatenate([rows[r * n + q * w:r * n + (q + 1) * w] for r in range(dilation)], axis=0)
        out.append(_dot(unperm, gathered))
    return jnp.concatenate(out, axis=0)


def _span_tiles(dilation):
    return max(1, ATTN_BLOCK * dilation // ROW_TILE)


def _grouped_view(a, dilation):
    t, n = a.shape
    k = _span_tiles(dilation)
    if k == 1:
        return a
    return a.reshape(t // (k * ROW_TILE), dilation, k, ROW_TILE // dilation, n)


def _grouped_spec(n, dilation):
    k = _span_tiles(dilation)
    if k == 1:
        return pl.BlockSpec((ROW_TILE, n), lambda i, *_: (i, 0))
    return pl.BlockSpec((None, dilation, None, ROW_TILE // dilation, n), lambda i, *_: (i // k, 0, i % k, 0, 0))


def _grouped_store(ref, dilation, val):
    if _span_tiles(dilation) == 1:
        ref[...] = val
    else:
        n = ROW_TILE // dilation
        for r in range(dilation):
            ref[r] = val[r * n:(r + 1) * n]


def _grouped_load(ref, dilation):
    if _span_tiles(dilation) == 1:
        return ref[...]
    return jnp.concatenate([ref[r] for r in range(dilation)], axis=0)


def _attn_qkv_kernel(dilation, x_ref, gain_ref, *rest):
    d = D_MODEL
    h = _rmsnorm_rows(x_ref[...], gain_ref[...]).astype(BF16)
    if dilation > 1:
        perm_ref, w_ref, o_ref = rest
        h = _to_group_order(h, perm_ref[...], dilation)
    else:
        w_ref, o_ref = rest
    res = _dot(h, w_ref[...])
    val = jnp.concatenate([(res[:, :d] * (ATTN_HEAD_DIM ** -0.5 * LOG2_E)).astype(BF16),
                           res[:, d:].astype(BF16)], axis=1)
    _grouped_store(o_ref, dilation, val)


def _attn_qkv(x2, gain, w, layer_j, group, dilation):
    t, d = x2.shape
    n = 3 * d
    perm = [jnp.asarray(_block_perm(dilation), BF16)] if dilation > 1 else []
    w_spec = _layer_spec(w, layer_j, group, n)
    out = pl.pallas_call(
        functools.partial(_attn_qkv_kernel, dilation),
        grid=(t // ROW_TILE,),
        in_specs=[pl.BlockSpec((ROW_TILE, d), lambda i: (i, 0)), _const_spec((1, d))]
        + [_const_spec(p.shape) for p in perm] + [w_spec],
        out_specs=_grouped_spec(n, dilation),
        out_shape=jax.ShapeDtypeStruct(_grouped_view(jnp.empty((t, n), BF16), dilation).shape, BF16),
        compiler_params=_params("arbitrary"),
        name=f"attn_qkv_d{dilation}",
    )(x2, gain, *perm, w)
    return out.reshape(t, n)


def _attn_group_kernel(dilation, blocks_per_seq, q_ref, kp_ref, kc_ref, vp_ref, vc_ref, lm_ref, sel_ref, bias_ref,
                       o_ref, lse_ref):
    blk = ATTN_BLOCK
    nb = ATTN_STEP_BLOCKS
    m_lo, m_hi = lm_ref[0], lm_ref[1]
    low_q = lax.broadcasted_iota(jnp.int32, (blk, LANES), 1) < ATTN_HEAD_DIM

    for j in range(nb):
        rows = slice(j * blk, (j + 1) * blk)
        if dilation >= nb:
            prev_k, prev_v, jp = kp_ref, vp_ref, j
        elif j >= dilation:
            prev_k, prev_v, jp = kc_ref, vc_ref, j - dilation
        else:
            prev_k, prev_v, jp = kp_ref, vp_ref, j - dilation + nb
        prev_rows = slice(jp * blk, (jp + 1) * blk)
        has_prev = (((pl.program_id(0) * nb + j) % blocks_per_seq) >= dilation).astype(jnp.int32)
        m_tile = jnp.zeros((blk, LANES), F32)
        l_tile = sel_ref[HEAD_PAIRS]

        for p in range(HEAD_PAIRS):
            sl = slice(p * LANES, (p + 1) * LANES)
            qp = q_ref[rows, sl]
            q_bd = jnp.concatenate([qp * m_lo[:blk], qp * m_hi[:blk]], axis=0)
            k_cat = jnp.concatenate([prev_k[prev_rows, sl], kc_ref[rows, sl]], axis=0)
            s = _dot_nt(q_bd, k_cat) + bias_ref[has_prev]
            s0, s1 = s[:blk], s[blk:]
            m0 = jnp.max(s0, axis=-1, keepdims=True)
            m1 = jnp.max(s1, axis=-1, keepdims=True)
            p_cat = jnp.concatenate([jnp.exp2(s0 - m0).astype(BF16), jnp.exp2(s1 - m1).astype(BF16)], axis=1)
            v_cat = jnp.concatenate([prev_v[prev_rows, sl], vc_ref[rows, sl]], axis=0)
            rhs = jnp.concatenate([jnp.concatenate([v_cat * m_lo, m_lo], axis=1),
                                   jnp.concatenate([v_cat * m_hi, m_hi], axis=1)], axis=0)
            res = _dot(p_cat, rhs)
            denom = res[:, LANES:]
            o_ref[rows, sl] = (res[:, :LANES] / denom).astype(o_ref.dtype)
            m_tile = m_tile + sel_ref[p] * jnp.where(low_q, m0, m1)
            l_tile = l_tile + sel_ref[p] * denom

        lse_ref[rows, :] = m_tile * LN_2 + jnp.log(l_tile)


def _attn_group(qkv, dilation, seq):
    t, n = qkv.shape
    d = D_MODEL
    nb = ATTN_STEP_BLOCKS
    blk = nb * ATTN_BLOCK
    assert (dilation % nb == 0 or nb % dilation == 0) and (seq // ATTN_BLOCK) % nb == 0
    back = max(dilation // nb, 1)

    def spec(which, prev):
        if prev:
            return pl.BlockSpec((blk, d), lambda i: (jnp.maximum(i - back, 0), which))
        return pl.BlockSpec((blk, d), lambda i: (i, which))

    first = np.broadcast_to(np.arange(LANES) < ATTN_HEAD_DIM, (2 * ATTN_BLOCK, LANES))
    lane_masks = jnp.asarray(np.stack([first, ~first]).astype(np.float32), BF16)
    select = np.zeros((HEAD_PAIRS + 1, ATTN_BLOCK, LANES), np.float32)
    for p in range(HEAD_PAIRS):
        select[p, :, [p, ATTN_HEAD_DIM + p]] = 1.0
    select[HEAD_PAIRS] = 1.0 - select[:HEAD_PAIRS].sum(axis=0)
    lane_select = jnp.asarray(select)
    a = np.arange(2 * ATTN_BLOCK)[:, None] % ATTN_BLOCK
    c = np.arange(2 * ATTN_BLOCK)[None, :]
    own = (c >= ATTN_BLOCK) & (c - ATTN_BLOCK <= a)
    earlier = (c < ATTN_BLOCK) & (c >= a)
    bias = jnp.asarray(np.where(np.stack([own, own | earlier]), 0.0, -np.inf).astype(np.float32))

    return pl.pallas_call(
        functools.partial(_attn_group_kernel, dilation, seq // ATTN_BLOCK),
        grid=(t // blk,),
        in_specs=[spec(0, False), spec(1, True), spec(1, False), spec(2, True), spec(2, False),
                  _const_spec(lane_masks.shape), _const_spec(lane_select.shape), _const_spec(bias.shape)],
        out_specs=[pl.BlockSpec((blk, d), lambda i: (i, 0)), pl.BlockSpec((blk, LANES), lambda i: (i, 0))],
        out_shape=[jax.ShapeDtypeStruct((t, d), BF16), jax.ShapeDtypeStruct((t, LANES), F32)],
        compiler_params=_params("arbitrary"),
        name=f"attn_group_d{dilation}",
    )(qkv, qkv, qkv, qkv, qkv, lane_masks, lane_select, bias)


def _head_expand_matrix():
    e = np.zeros((LANES, D_MODEL), np.float32)
    for h in range(ATTN_HEADS):
        lane = (h // 2) + ATTN_HEAD_DIM * (h % 2)
        e[lane, h * ATTN_HEAD_DIM:(h + 1) * ATTN_HEAD_DIM] = 1.0
    return jnp.asarray(e, BF16)


def _mlp_tail(x1, gain_ref, w1_ref, w2_ref, final_gain_ref, o_ref):
    d = D_MODEL
    h = _rmsnorm_rows(x1, gain_ref[...]).astype(BF16)
    acc = x1
    for c in range(MLP_HIDDEN // d):
        a = jnp.maximum(_dot(h, w1_ref[:, c * d:(c + 1) * d]), 0.0)
        acc = acc + _dot((a * a).astype(BF16), w2_ref[c * d:(c + 1) * d, :])
    if final_gain_ref is not None:
        acc = _rmsnorm_rows(acc, final_gain_ref[...])
    o_ref[...] = acc


def _post_kernel(final, x_ref, y_ref, wo_ref, gain_ref, w1_ref, w2_ref, *rest):
    if final:
        final_gain_ref, o_ref = rest
    else:
        final_gain_ref, (o_ref,) = None, rest
    x1 = x_ref[...] + _dot(y_ref[...], wo_ref[...])
    _mlp_tail(x1, gain_ref, w1_ref, w2_ref, final_gain_ref, o_ref)


def _attn_post_kernel(final, x_ref, o0_ref, o1_ref, o2_ref, l0_ref, l1_ref, l2_ref, u1_ref, u2_ref, exp_ref,
                      wo_ref, gain_ref, w1_ref, w2_ref, *rest):
    if final:
        final_gain_ref, o_ref = rest
    else:
        final_gain_ref, (o_ref,) = None, rest
    dils = [dil for _, dil in DILATED_GROUPS]
    outs = [o0_ref[...]]
    lses = [l0_ref[...]]
    for o_g, l_g, u_ref, dil in zip((o1_ref, o2_ref), (l1_ref, l2_ref), (u1_ref, u2_ref), dils[1:]):
        outs.append(_to_natural_order(_grouped_load(o_g, dil), u_ref[...], dil).astype(BF16))
        both = _to_natural_order(jnp.concatenate(_split_bf16(_grouped_load(l_g, dil)), axis=1), u_ref[...], dil)
        lses.append(both[:, :LANES] + both[:, LANES:])
    m = jnp.maximum(jnp.maximum(lses[0], lses[1]), lses[2])
    es = [jnp.exp(l - m) for l in lses]
    inv = 1.0 / (es[0] + es[1] + es[2])
    y = None
    for e, o_nat in zip(es, outs):
        term = _dot((e * inv).astype(BF16), exp_ref[...]) * o_nat.astype(F32)
        y = term if y is None else y + term
    x1 = x_ref[...] + _dot(y.astype(BF16), wo_ref[...])
    _mlp_tail(x1, gain_ref, w1_ref, w2_ref, final_gain_ref, o_ref)


def _post(x2, mix, wo, layer_j, gain, w1, w2, layer, final_gain):
    t, d = x2.shape
    assert t % ROW_TILE == 0
    row = lambda i: (i, 0)
    final = final_gain is not None
    weights = [wo, gain, w1, w2] + ([final_gain] if final else [])
    weight_specs = ([_layer_spec(wo, layer_j), _const_spec(gain.shape), _layer_spec(w1, layer), _layer_spec(w2, layer)]
                    + ([_const_spec(final_gain.shape)] if final else []))
    if isinstance(mix, tuple):
        dils = [dil for _, dil in DILATED_GROUPS]
        unperms = [jnp.asarray(_block_perm(dil).T, BF16) for dil in dils[1:]]
        expand = _head_expand_matrix()
        body = functools.partial(_attn_post_kernel, final)
        views = [_grouped_view(a, dil) for a, dil in zip(mix, dils + dils)]
        args = [x2, *views, *unperms, expand] + weights
        in_specs = ([pl.BlockSpec((ROW_TILE, d), row)] + [_grouped_spec(d, dil) for dil in dils]
                    + [_grouped_spec(LANES, dil) for dil in dils]
                    + [_const_spec(u.shape) for u in unperms] + [_const_spec(expand.shape)] + weight_specs)
    else:
        body = functools.partial(_post_kernel, final)
        args = [x2, mix] + weights
        in_specs = [pl.BlockSpec((ROW_TILE, d), row)] * 2 + weight_specs
    return pl.pallas_call(
        body,
        grid=(t // ROW_TILE,),
        in_specs=in_specs,
        out_specs=pl.BlockSpec((ROW_TILE, d), row),
        out_shape=jax.ShapeDtypeStruct((t, d), F32),
        compiler_params=_params("arbitrary"),
        name="post_mlp",
    )(*args)


def kernel(x, norm_mix, norm_mlp, norm_final, hgrn_w_in, hgrn_lower_bound, hgrn_out_norm, hgrn_w_out,
           attn_w_qkv, attn_w_out, mlp_w1, mlp_w2):
    b, s, d = x.shape
    depth = norm_mix.shape[0]
    n_mixers = 2
    x2 = x.reshape(b * s, d)
    row = lambda v: v.reshape(1, -1).astype(F32)
    lbp = hgrn_lower_bound.astype(F32)
    hgrn_w_in, hgrn_w_out, attn_w_qkv, attn_w_out, mlp_w1, mlp_w2 = (
        w.astype(BF16) for w in (hgrn_w_in, hgrn_w_out, attn_w_qkv, attn_w_out, mlp_w1, mlp_w2))

    for layer in range(depth):
        j = layer // n_mixers
        final_gain = row(norm_final) if layer == depth - 1 else None
        if layer % n_mixers == 0:
            q, k, v, lf, sg = _hgrn_pre(x2, row(norm_mix[layer]), hgrn_w_in, lbp, j)
            to3 = lambda a: a.reshape(b, s, d)
            mix = _hgrn_scan(to3(q), to3(k), to3(v), to3(lf), to3(sg), row(hgrn_out_norm[j])).reshape(b * s, d)
            wo = hgrn_w_out
        else:
            outs, lses = [], []
            for g, (window, dilation) in enumerate(DILATED_GROUPS):
                assert window // dilation == ATTN_BLOCK and s % (ATTN_BLOCK * dilation) == 0
                qkv = _attn_qkv(x2, row(norm_mix[layer]), attn_w_qkv, j, g, dilation)
                o_g, lse_g = _attn_group(qkv, dilation, s)
                outs.append(o_g)
                lses.append(lse_g)
            mix = (*outs, *lses)
            wo = attn_w_out
        x2 = _post(x2, mix, wo, j, row(norm_mlp[layer]), mlp_w1, mlp_w2, layer, final_gain)
    return x2.reshape(b, s, d)
```

```python
import functools

import numpy as np
import jax
import jax.numpy as jnp
from jax import lax
from jax.experimental import pallas as pl
from jax.experimental.pallas import tpu as pltpu

F32 = jnp.float32
BF16 = jnp.bfloat16

LOG2_E = 1.4426950408889634
LN_2 = 0.6931471805599453

D_MODEL = 1024
NORM_EPS = 1e-6
MLP_HIDDEN = 4 * D_MODEL

HGRN_HEAD_DIM = 128
HGRN_HEADS = D_MODEL // HGRN_HEAD_DIM
HGRN_CHUNK = 128
HGRN_STEP_CHUNKS = 4

ATTN_HEADS = 16
ATTN_HEAD_DIM = D_MODEL // ATTN_HEADS
DILATED_GROUPS = ((128, 1), (512, 4), (2048, 16))
N_GROUPS = len(DILATED_GROUPS)
ATTN_BLOCK = 128
ATTN_STEP_BLOCKS = 8
LANES = 128
BF16_ROWS = 16
HGRN_HALF_ROWS = BF16_ROWS
HGRN_COMPACT_FROM = BF16_ROWS
HEAD_PAIRS = ATTN_HEADS // 2

VMEM_LIMIT_BYTES = 56 * 1024 * 1024

ROW_TILE = 512
PRE_ROWS = 1024
PRE_COLS = 256


def _dot(a, b):
    return jnp.dot(a, b, preferred_element_type=F32)


def _dot_nt(a, b):
    return lax.dot_general(a, b, (((1,), (1,)), ((), ())), preferred_element_type=F32)


def _dot_tn(a, b):
    return lax.dot_general(a, b, (((0,), (0,)), ((), ())), preferred_element_type=F32)


def _split_bf16(x):
    hi = x.astype(BF16)
    return hi, (x - hi.astype(F32)).astype(BF16)


def _rmsnorm_rows(x, gain):
    return x * lax.rsqrt(jnp.mean(x * x, axis=-1, keepdims=True) + NORM_EPS) * gain


def _params(*semantics):
    return pltpu.CompilerParams(dimension_semantics=semantics, vmem_limit_bytes=VMEM_LIMIT_BYTES)


def _const_spec(shape):
    zeros = (0,) * len(shape)
    return pl.BlockSpec(shape, lambda *_: zeros, pipeline_mode=pl.Buffered(1))


def _layer_spec(stack, layer, col_block=0, cols=None):
    _, rows, n = stack.shape
    return pl.BlockSpec((None, rows, n if cols is None else cols), lambda *_: (layer, 0, col_block),
                        pipeline_mode=pl.Buffered(1))


def _hgrn_pre_kernel(layer_j, x_ref, gain_ref, w_ref, lbp_ref, q_ref, k_ref, v_ref, lf_ref, sg_ref):
    d = D_MODEL
    h = _rmsnorm_rows(x_ref[...], gain_ref[...]).astype(BF16)

    if layer_j > 0:
        p = lbp_ref[...]
        e = jnp.exp(p - jnp.max(p, axis=0, keepdims=True))
        sm = e / jnp.sum(e, axis=0, keepdims=True)
        lb = sm[1:2, :]
        for i in range(2, layer_j + 1):
            lb = lb + sm[i:i + 1, :]
        log_lb = jnp.log2(lb)
        log_1m_lb = jnp.log2(1.0 - lb)

    for j in range(d // PRE_COLS):
        out = slice(j * PRE_COLS, (j + 1) * PRE_COLS)
        proj = lambda blk: _dot(h, w_ref[:, blk * d + j * PRE_COLS:blk * d + (j + 1) * PRE_COLS])

        q = proj(0)
        q_ref[:, out] = (q / (1.0 + jnp.exp2(q * (-LOG2_E)))).astype(BF16)

        f = proj(1)
        f2 = f * LOG2_E
        ef = jnp.exp2(-jnp.abs(f2))
        inv = 1.0 / (1.0 + ef)
        log_sig = jnp.minimum(f2, 0.0) - jnp.log2(1.0 + ef)
        sig_neg = jnp.where(f >= 0.0, ef * inv, inv)
        if layer_j == 0:
            lf_ref[:, out] = log_sig
            k_ref[:, out] = sig_neg.astype(BF16)
        else:
            a = log_lb[:, out]
            b = log_1m_lb[:, out] + log_sig
            lf_ref[:, out] = jnp.maximum(a, b) + jnp.log2(1.0 + jnp.exp2(-jnp.abs(a - b)))
            k_ref[:, out] = ((1.0 - lb[:, out]) * sig_neg).astype(BF16)

        v_ref[:, out] = proj(2).astype(BF16)

        g = proj(3)
        sg_ref[:, out] = (g / (1.0 + jnp.exp2(g * (-LOG2_E)))).astype(BF16)


def _hgrn_pre(x2, gain, w_in, lbp, layer_j):
    t, d = x2.shape
    tm = min(PRE_ROWS, t)
    row = lambda i: (i, 0)
    out_bf = jax.ShapeDtypeStruct((t, d), BF16)
    return pl.pallas_call(
        functools.partial(_hgrn_pre_kernel, layer_j),
        grid=(t // tm,),
        in_specs=[pl.BlockSpec((tm, d), row), _const_spec((1, d)), _layer_spec(w_in, layer_j),
                  _const_spec(lbp.shape)],
        out_specs=[pl.BlockSpec((tm, d), row)] * 5,
        out_shape=[out_bf, out_bf, out_bf, jax.ShapeDtypeStruct((t, d), F32), out_bf],
        compiler_params=_params("arbitrary"),
        name="hgrn_pre",
    )(x2, gain, w_in, lbp)


def _hgrn_levels(c):
    m, out = 1, []
    while m < c:
        out.append(m)
        m *= 2
    return out


def _hgrn_tables(c):
    levels = _hgrn_levels(c)
    idx = np.arange(c)
    x, j = idx[:, None], idx[None, :]
    blocks = []
    masks = [(x == j)]
    for m in levels:
        bnd = (x // (2 * m)) * (2 * m) + m - 1
        second = (x % (2 * m)) >= m
        if m > 1:
            blocks.append(np.where(second, (j > bnd) & (j <= x), (j > x) & (j <= bnd)))
        t, s = x, j
        mask = (t // (2 * m) == s // (2 * m)) & ((t % (2 * m)) >= m) & ((s % (2 * m)) < m)
        if m < HGRN_COMPACT_FROM:
            masks.append(mask)
        elif c // (2 * m) > 1:
            masks.append(mask[second[:, 0]])
    blocks.append(np.broadcast_to(idx[None, :] < c // 2, (HGRN_HALF_ROWS, c)))
    a = np.concatenate(blocks, axis=0).astype(np.float32)
    a2 = np.concatenate([a, a], axis=1)
    masks = np.concatenate(masks, axis=0).astype(np.float32)
    masks2 = np.concatenate([masks, masks], axis=1)
    return jnp.asarray(a2, BF16), jnp.asarray(masks2)


def _hgrn_scan_kernel(q_ref, k_ref, v_ref, lf_ref, sg_ref, a2_ref, mask_ref, onorm_ref, y_ref,
                      state_ref, g_ref):
    c = HGRN_CHUNK
    hd = HGRN_HEAD_DIM
    levels = _hgrn_levels(c)

    @pl.when(pl.program_id(1) == 0)
    def _():
        state_ref[...] = jnp.zeros_like(state_ref)

    for ci in range(HGRN_STEP_CHUNKS):
        lf_hi, lf_lo = _split_bf16(lf_ref[0, ci * c:(ci + 1) * c])
        g_ref[ci] = _dot(a2_ref[...], jnp.concatenate([lf_hi, lf_lo], axis=0))

    zeros = jnp.zeros((c, hd), BF16)
    zeros2 = jnp.zeros((c, 2 * hd), BF16)

    def block_diag(x):
        return jnp.concatenate([jnp.concatenate([x[:, :hd], zeros[:x.shape[0]]], axis=1),
                                jnp.concatenate([zeros[:x.shape[0]], x[:, hd:]], axis=1)], axis=0)

    def scaled(x_bf, log2_decay):
        return x_bf * jnp.exp2(log2_decay).astype(BF16)

    def decayed(x_bf, block, rows=slice(0, c)):
        return scaled(x_bf[rows], g_ref[ci, block * c + rows.start:block * c + rows.stop, sl])

    tile = BF16_ROWS
    half = c // 2
    for ci, p in [(ci, p) for ci in range(HGRN_STEP_CHUNKS) for p in range(HGRN_HEADS // 2)]:
        tok = slice(ci * c, (ci + 1) * c)
        sl = slice(2 * p * hd, 2 * (p + 1) * hd)
        q_bf = q_ref[0, tok, sl]
        k_bf = k_ref[0, tok, sl]
        v_bf = v_ref[0, tok, sl]
        st = [state_ref[2 * p], state_ref[2 * p + 1]]
        st_bd = block_diag(jnp.concatenate([st[0].astype(BF16), st[1].astype(BF16)], axis=1))

        n_sum = len(levels) - 1
        top = (n_sum - 1) * c
        g_lo = g_ref[ci, top:top + half, sl]
        g_hi = g_ref[ci, top + half:top + c, sl]
        first_half = g_ref[ci, n_sum * c:n_sum * c + 1, sl]
        g_last = g_hi[half - 1:half]
        since_start = jnp.concatenate([first_half - g_lo, first_half + g_hi], axis=0)
        until_end = jnp.concatenate([g_last + g_lo, g_last - g_hi], axis=0)

        o = _dot_nt(scaled(q_bf, since_start), st_bd)

        pieces = [None] * (c // tile)

        def add_rows(start, val):
            for i in range(val.shape[0] // tile):
                piece = val[i * tile:(i + 1) * tile]
                at = start // tile + i
                pieces[at] = piece if pieces[at] is None else pieces[at] + piece

        qk = q_bf.astype(F32) * k_bf.astype(F32)
        diag = jnp.concatenate(
            [jnp.broadcast_to(jnp.sum(qk[:, i * hd:(i + 1) * hd], axis=-1, keepdims=True), (c, c)) for i in range(2)],
            axis=1)
        add_rows(0, mask_ref[0:c] * diag)
        k_up = pltpu.roll(k_bf.astype(F32), 1, axis=0)
        qfk = q_bf.astype(F32) * jnp.exp2(lf_ref[0, tok, sl]) * k_up
        sub = jnp.concatenate(
            [jnp.broadcast_to(jnp.sum(qfk[:, i * hd:(i + 1) * hd], axis=-1, keepdims=True), (c, c)) for i in range(2)],
            axis=1)
        add_rows(0, mask_ref[c:2 * c] * sub)
        mrow = 2 * c
        for li, m in enumerate(levels[1:]):
            blk = li
            if m < HGRN_COMPACT_FROM:
                s = _dot_nt(decayed(q_bf, blk), block_diag(decayed(k_bf, blk)))
                add_rows(0, mask_ref[mrow:mrow + c] * s)
                mrow += c
            else:
                n_blk = c // (2 * m)
                first = [slice(b * 2 * m, b * 2 * m + m) for b in range(n_blk)]
                second = [slice(b * 2 * m + m, (b + 1) * 2 * m) for b in range(n_blk)]
                lhs = jnp.concatenate([decayed(q_bf, blk, r) for r in second], axis=0)
                keys = jnp.concatenate([x for r in first for x in (decayed(k_bf, blk, r), zeros2[:m])], axis=0)
                s = _dot_nt(lhs, block_diag(keys))
                if n_blk > 1:
                    s = mask_ref[mrow:mrow + c // 2] * s
                    mrow += c // 2
                for b, r in enumerate(second):
                    add_rows(r.start, s[b * m:(b + 1) * m])
        scores = jnp.concatenate(pieces, axis=0)
        o = o + _dot(scores.astype(BF16), block_diag(v_bf))

        d_state = _dot_tn(v_bf, scaled(k_bf, until_end))
        decay_all = jnp.exp2(first_half + g_last)
        for i in range(2):
            hs = slice(i * hd, (i + 1) * hd)
            state_ref[2 * p + i] = st[i] * decay_all[:, hs] + d_state[hs, hs]
            oh = o[:, hs]
            oh = oh * lax.rsqrt(jnp.mean(oh * oh, axis=-1, keepdims=True) + NORM_EPS * hd)
            lanes = slice((2 * p + i) * hd, (2 * p + i + 1) * hd)
            y_ref[0, tok, lanes] = (oh * onorm_ref[:, lanes] * sg_ref[0, tok, lanes].astype(F32)).astype(BF16)


def _hgrn_scan(q, k, v, lf, sg, out_norm):
    b, s, d = q.shape
    c = HGRN_CHUNK
    a2, masks = _hgrn_tables(c)
    rows = HGRN_STEP_CHUNKS * c
    assert s % rows == 0
    tok = pl.BlockSpec((1, rows, d), lambda i, j: (i, j, 0))
    return pl.pallas_call(
        _hgrn_scan_kernel,
        grid=(b, s // rows),
        in_specs=[tok, tok, tok, tok, tok, _const_spec(a2.shape), _const_spec(masks.shape),
                  _const_spec((1, d))],
        out_specs=tok,
        out_shape=jax.ShapeDtypeStruct((b, s, d), BF16),
        scratch_shapes=[pltpu.VMEM((HGRN_HEADS, HGRN_HEAD_DIM, HGRN_HEAD_DIM), F32),
                        pltpu.VMEM((HGRN_STEP_CHUNKS, a2.shape[0], d), F32)],
        compiler_params=_params("arbitrary", "arbitrary"),
        name="hgrn_scan",
    )(q, k, v, lf, sg, a2, masks, out_norm)


def _perm_block(dilation):
    return max(ATTN_BLOCK, BF16_ROWS * dilation)


def _block_perm(dilation):
    bn = _perm_block(dilation)
    w = bn // dilation
    p = np.zeros((bn, bn), np.float32)
    for r in range(dilation):
        for i in range(w):
            p[r * w + i, i * dilation + r] = 1.0
    return p


def _to_group_order(rows, perm, dilation):
    bn = _perm_block(dilation)
    w = bn // dilation
    blocks = [_dot(perm, rows[q * bn:(q + 1) * bn]).astype(rows.dtype) for q in range(ROW_TILE // bn)]
    return jnp.concatenate([blk[r * w:(r + 1) * w] for r in range(dilation) for blk in blocks], axis=0)


def _to_natural_order(rows, unperm, dilation):
    bn = _perm_block(dilation)
    w = bn // dilation
    n = ROW_TILE // dilation
    out = []
    for q in range(ROW_TILE // bn):
        gathered = jnp.concatenate([rows[r * n + q * w:r * n + (q + 1) * w] for r in range(dilation)], axis=0)
        out.append(_dot(unperm, gathered))
    return jnp.concatenate(out, axis=0)


def _span_tiles(dilation):
    return max(1, ATTN_BLOCK * dilation // ROW_TILE)


def _grouped_view(a, dilation):
    t, n = a.shape
    k = _span_tiles(dilation)
    if k == 1:
        return a
    return a.reshape(t // (k * ROW_TILE), dilation, k, ROW_TILE // dilation, n)


def _grouped_spec(n, dilation):
    k = _span_tiles(dilation)
    if k == 1:
        return pl.BlockSpec((ROW_TILE, n), lambda i, *_: (i, 0))
    return pl.BlockSpec((None, dilation, None, ROW_TILE // dilation, n), lambda i, *_: (i // k, 0, i % k, 0, 0))


def _grouped_store(ref, dilation, val):
    if _span_tiles(dilation) == 1:
        ref[...] = val
    else:
        n = ROW_TILE // dilation
        for r in range(dilation):
            ref[r] = val[r * n:(r + 1) * n]


def _grouped_load(ref, dilation):
    if _span_tiles(dilation) == 1:
        return ref[...]
    return jnp.concatenate([ref[r] for r in range(dilation)], axis=0)


def _attn_qkv_kernel(dilation, x_ref, gain_ref, *rest):
    d = D_MODEL
    h = _rmsnorm_rows(x_ref[...], gain_ref[...]).astype(BF16)
    if dilation > 1:
        perm_ref, w_ref, o_ref = rest
        h = _to_group_order(h, perm_ref[...], dilation)
    else:
        w_ref, o_ref = rest
    res = _dot(h, w_ref[...])
    val = jnp.concatenate([(res[:, :d] * (ATTN_HEAD_DIM ** -0.5 * LOG2_E)).astype(BF16),
                           res[:, d:].astype(BF16)], axis=1)
    _grouped_store(o_ref, dilation, val)


def _attn_qkv(x2, gain, w, layer_j, group, dilation):
    t, d = x2.shape
    n = 3 * d
    perm = [jnp.asarray(_block_perm(dilation), BF16)] if dilation > 1 else []
    w_spec = _layer_spec(w, layer_j, group, n)
    out = pl.pallas_call(
        functools.partial(_attn_qkv_kernel, dilation),
        grid=(t // ROW_TILE,),
        in_specs=[pl.BlockSpec((ROW_TILE, d), lambda i: (i, 0)), _const_spec((1, d))]
        + [_const_spec(p.shape) for p in perm] + [w_spec],
        out_specs=_grouped_spec(n, dilation),
        out_shape=jax.ShapeDtypeStruct(_grouped_view(jnp.empty((t, n), BF16), dilation).shape, BF16),
        compiler_params=_params("arbitrary"),
        name=f"attn_qkv_d{dilation}",
    )(x2, gain, *perm, w)
    return out.reshape(t, n)


def _attn_group_kernel(dilation, blocks_per_seq, q_ref, kp_ref, kc_ref, vp_ref, vc_ref, lm_ref, sel_ref, bias_ref,
                       o_ref, lse_ref):
    blk = ATTN_BLOCK
    nb = ATTN_STEP_BLOCKS
    m_lo, m_hi = lm_ref[0], lm_ref[1]
    low_q = lax.broadcasted_iota(jnp.int32, (blk, LANES), 1) < ATTN_HEAD_DIM

    for j in range(nb):
        rows = slice(j * blk, (j + 1) * blk)
        if dilation >= nb:
            prev_k, prev_v, jp = kp_ref, vp_ref, j
        elif j >= dilation:
            prev_k, prev_v, jp = kc_ref, vc_ref, j - dilation
        else:
            prev_k, prev_v, jp = kp_ref, vp_ref, j - dilation + nb
        prev_rows = slice(jp * blk, (jp + 1) * blk)
        has_prev = (((pl.program_id(0) * nb + j) % blocks_per_seq) >= dilation).astype(jnp.int32)
        m_tile = jnp.zeros((blk, LANES), F32)
        l_tile = sel_ref[HEAD_PAIRS]

        for p in range(HEAD_PAIRS):
            sl = slice(p * LANES, (p + 1) * LANES)
            qp = q_ref[rows, sl]
            q_bd = jnp.concatenate([qp * m_lo[:blk], qp * m_hi[:blk]], axis=0)
            k_cat = jnp.concatenate([prev_k[prev_rows, sl], kc_ref[rows, sl]], axis=0)
            s = _dot_nt(q_bd, k_cat) + bias_ref[has_prev]
            s0, s1 = s[:blk], s[blk:]
            m0 = jnp.max(s0, axis=-1, keepdims=True)
            m1 = jnp.max(s1, axis=-1, keepdims=True)
            p_cat = jnp.concatenate([jnp.exp2(s0 - m0).astype(BF16), jnp.exp2(s1 - m1).astype(BF16)], axis=1)
            v_cat = jnp.concatenate([prev_v[prev_rows, sl], vc_ref[rows, sl]], axis=0)
            rhs = jnp.concatenate([jnp.concatenate([v_cat * m_lo, m_lo], axis=1),
                                   jnp.concatenate([v_cat * m_hi, m_hi], axis=1)], axis=0)
            res = _dot(p_cat, rhs)
            denom = res[:, LANES:]
            o_ref[rows, sl] = (res[:, :LANES] / denom).astype(o_ref.dtype)
            m_tile = m_tile + sel_ref[p] * jnp.where(low_q, m0, m1)
            l_tile = l_tile + sel_ref[p] * denom

        lse_ref[rows, :] = m_tile * LN_2 + jnp.log(l_tile)


def _attn_group(qkv, dilation, seq):
    t, n = qkv.shape
    d = D_MODEL
    nb = ATTN_STEP_BLOCKS
    blk = nb * ATTN_BLOCK
    assert (dilation % nb == 0 or nb % dilation == 0) and (seq // ATTN_BLOCK) % nb == 0
    back = max(dilation // nb, 1)

    def spec(which, prev):
        if prev:
            return pl.BlockSpec((blk, d), lambda i: (jnp.maximum(i - back, 0), which))
        return pl.BlockSpec((blk, d), lambda i: (i, which))

    first = np.broadcast_to(np.arange(LANES) < ATTN_HEAD_DIM, (2 * ATTN_BLOCK, LANES))
    lane_masks = jnp.asarray(np.stack([first, ~first]).astype(np.float32), BF16)
    select = np.zeros((HEAD_PAIRS + 1, ATTN_BLOCK, LANES), np.float32)
    for p in range(HEAD_PAIRS):
        select[p, :, [p, ATTN_HEAD_DIM + p]] = 1.0
    select[HEAD_PAIRS] = 1.0 - select[:HEAD_PAIRS].sum(axis=0)
    lane_select = jnp.asarray(select)
    a = np.arange(2 * ATTN_BLOCK)[:, None] % ATTN_BLOCK
    c = np.arange(2 * ATTN_BLOCK)[None, :]
    own = (c >= ATTN_BLOCK) & (c - ATTN_BLOCK <= a)
    earlier = (c < ATTN_BLOCK) & (c >= a)
    bias = jnp.asarray(np.where(np.stack([own, own | earlier]), 0.0, -np.inf).astype(np.float32))

    return pl.pallas_call(
        functools.partial(_attn_group_kernel, dilation, seq // ATTN_BLOCK),
        grid=(t // blk,),
        in_specs=[spec(0, False), spec(1, True), spec(1, False), spec(2, True), spec(2, False),
                  _const_spec(lane_masks.shape), _const_spec(lane_select.shape), _const_spec(bias.shape)],
        out_specs=[pl.BlockSpec((blk, d), lambda i: (i, 0)), pl.BlockSpec((blk, LANES), lambda i: (i, 0))],
        out_shape=[jax.ShapeDtypeStruct((t, d), BF16), jax.ShapeDtypeStruct((t, LANES), F32)],
        compiler_params=_params("arbitrary"),
        name=f"attn_group_d{dilation}",
    )(qkv, qkv, qkv, qkv, qkv, lane_masks, lane_select, bias)


def _head_expand_matrix():
    e = np.zeros((LANES, D_MODEL), np.float32)
    for h in range(ATTN_HEADS):
        lane = (h // 2) + ATTN_HEAD_DIM * (h % 2)
        e[lane, h * ATTN_HEAD_DIM:(h + 1) * ATTN_HEAD_DIM] = 1.0
    return jnp.asarray(e, BF16)


def _mlp_tail(x1, gain_ref, w1_ref, w2_ref, final_gain_ref, o_ref):
    d = D_MODEL
    h = _rmsnorm_rows(x1, gain_ref[...]).astype(BF16)
    acc = x1
    for c in range(MLP_HIDDEN // d):
        a = jnp.maximum(_dot(h, w1_ref[:, c * d:(c + 1) * d]), 0.0)
        acc = acc + _dot((a * a).astype(BF16), w2_ref[c * d:(c + 1) * d, :])
    if final_gain_ref is not None:
        acc = _rmsnorm_rows(acc, final_gain_ref[...])
    o_ref[...] = acc


def _post_kernel(final, x_ref, y_ref, wo_ref, gain_ref, w1_ref, w2_ref, *rest):
    if final:
        final_gain_ref, o_ref = rest
    else:
        final_gain_ref, (o_ref,) = None, rest
    x1 = x_ref[...] + _dot(y_ref[...], wo_ref[...])
    _mlp_tail(x1, gain_ref, w1_ref, w2_ref, final_gain_ref, o_ref)


def _attn_post_kernel(final, x_ref, o0_ref, o1_ref, o2_ref, l0_ref, l1_ref, l2_ref, u1_ref, u2_ref, exp_ref,
                      wo_ref, gain_ref, w1_ref, w2_ref, *rest):
    if final:
        final_gain_ref, o_ref = rest
    else:
        final_gain_ref, (o_ref,) = None, rest
    dils = [dil for _, dil in DILATED_GROUPS]
    outs = [o0_ref[...]]
    lses = [l0_ref[...]]
    for o_g, l_g, u_ref, dil in zip((o1_ref, o2_ref), (l1_ref, l2_ref), (u1_ref, u2_ref), dils[1:]):
        outs.append(_to_natural_order(_grouped_load(o_g, dil), u_ref[...], dil).astype(BF16))
        both = _to_natural_order(jnp.concatenate(_split_bf16(_grouped_load(l_g, dil)), axis=1), u_ref[...], dil)
        lses.append(both[:, :LANES] + both[:, LANES:])
    m = jnp.maximum(jnp.maximum(lses[0], lses[1]), lses[2])
    es = [jnp.exp(l - m) for l in lses]
    inv = 1.0 / (es[0] + es[1] + es[2])
    y = None
    for e, o_nat in zip(es, outs):
        term = _dot((e * inv).astype(BF16), exp_ref[...]) * o_nat.astype(F32)
        y = term if y is None else y + term
    x1 = x_ref[...] + _dot(y.astype(BF16), wo_ref[...])
    _mlp_tail(x1, gain_ref, w1_ref, w2_ref, final_gain_ref, o_ref)


def _post(x2, mix, wo, layer_j, gain, w1, w2, layer, final_gain):
    t, d = x2.shape
    assert t % ROW_TILE == 0
    row = lambda i: (i, 0)
    final = final_gain is not None
    weights = [wo, gain, w1, w2] + ([final_gain] if final else [])
    weight_specs = ([_layer_spec(wo, layer_j), _const_spec(gain.shape), _layer_spec(w1, layer), _layer_spec(w2, layer)]
                    + ([_const_spec(final_gain.shape)] if final else []))
    if isinstance(mix, tuple):
        dils = [dil for _, dil in DILATED_GROUPS]
        unperms = [jnp.asarray(_block_perm(dil).T, BF16) for dil in dils[1:]]
        expand = _head_expand_matrix()
        body = functools.partial(_attn_post_kernel, final)
        views = [_grouped_view(a, dil) for a, dil in zip(mix, dils + dils)]
        args = [x2, *views, *unperms, expand] + weights
        in_specs = ([pl.BlockSpec((ROW_TILE, d), row)] + [_grouped_spec(d, dil) for dil in dils]
                    + [_grouped_spec(LANES, dil) for dil in dils]
                    + [_const_spec(u.shape) for u in unperms] + [_const_spec(expand.shape)] + weight_specs)
    else:
        body = functools.partial(_post_kernel, final)
        args = [x2, mix] + weights
        in_specs = [pl.BlockSpec((ROW_TILE, d), row)] * 2 + weight_specs
    return pl.pallas_call(
        body,
        grid=(t // ROW_TILE,),
        in_specs=in_specs,
        out_specs=pl.BlockSpec((ROW_TILE, d), row),
        out_shape=jax.ShapeDtypeStruct((t, d), F32),
        compiler_params=_params("arbitrary"),
        name="post_mlp",
    )(*args)


def kernel(x, norm_mix, norm_mlp, norm_final, hgrn_w_in, hgrn_lower_bound, hgrn_out_norm, hgrn_w_out,
           attn_w_qkv, attn_w_out, mlp_w1, mlp_w2):
    b, s, d = x.shape
    depth = norm_mix.shape[0]
    n_mixers = 2
    x2 = x.reshape(b * s, d)
    row = lambda v: v.reshape(1, -1).astype(F32)
    lbp = hgrn_lower_bound.astype(F32)
    hgrn_w_in, hgrn_w_out, attn_w_qkv, attn_w_out, mlp_w1, mlp_w2 = (
        w.astype(BF16) for w in (hgrn_w_in, hgrn_w_out, attn_w_qkv, attn_w_out, mlp_w1, mlp_w2))

    for layer in range(depth):
        j = layer // n_mixers
        final_gain = row(norm_final) if layer == depth - 1 else None
        if layer % n_mixers == 0:
            q, k, v, lf, sg = _hgrn_pre(x2, row(norm_mix[layer]), hgrn_w_in, lbp, j)
            to3 = lambda a: a.reshape(b, s, d)
            mix = _hgrn_scan(to3(q), to3(k), to3(v), to3(lf), to3(sg), row(hgrn_out_norm[j])).reshape(b * s, d)
            wo = hgrn_w_out
        else:
            outs, lses = [], []
            for g, (window, dilation) in enumerate(DILATED_GROUPS):
                assert window // dilation == ATTN_BLOCK and s % (ATTN_BLOCK * dilation) == 0
                qkv = _attn_qkv(x2, row(norm_mix[layer]), attn_w_qkv, j, g, dilation)
                o_g, lse_g = _attn_group(qkv, dilation, s)
                outs.append(o_g)
                lses.append(lse_g)
            mix = (*outs, *lses)
            wo = attn_w_out
        x2 = _post(x2, mix, wo, j, row(norm_mlp[layer]), mlp_w1, mlp_w2, layer, final_gain)
    return x2.reshape(b, s, d)
```

```python
import functools

import numpy as np
import jax
import jax.numpy as jnp
from jax import lax
from jax.experimental import pallas as pl
from jax.experimental.pallas import tpu as pltpu

F32 = jnp.float32
BF16 = jnp.bfloat16

LOG2_E = 1.4426950408889634
LN_2 = 0.6931471805599453

D_MODEL = 1024
NORM_EPS = 1e-6
MLP_HIDDEN = 4 * D_MODEL

HGRN_HEAD_DIM = 128
HGRN_HEADS = D_MODEL // HGRN_HEAD_DIM
HGRN_CHUNK = 128
HGRN_STEP_CHUNKS = 4

ATTN_HEADS = 16
ATTN_HEAD_DIM = D_MODEL // ATTN_HEADS
DILATED_GROUPS = ((128, 1), (512, 4), (2048, 16))
N_GROUPS = len(DILATED_GROUPS)
ATTN_BLOCK = 128
ATTN_STEP_BLOCKS = 8
LANES = 128
BF16_ROWS = 16
HGRN_HALF_ROWS = BF16_ROWS
HGRN_COMPACT_FROM = BF16_ROWS
HEAD_PAIRS = ATTN_HEADS // 2

VMEM_LIMIT_BYTES = 56 * 1024 * 1024

ROW_TILE = 512
PRE_ROWS = 1024
PRE_COLS = 256


def _dot(a, b):
    return jnp.dot(a, b, preferred_element_type=F32)


def _dot_nt(a, b):
    return lax.dot_general(a, b, (((1,), (1,)), ((), ())), preferred_element_type=F32)


def _dot_tn(a, b):
    return lax.dot_general(a, b, (((0,), (0,)), ((), ())), preferred_element_type=F32)


def _split_bf16(x):
    hi = x.astype(BF16)
    return hi, (x - hi.astype(F32)).astype(BF16)


def _rmsnorm_rows(x, gain):
    return x * lax.rsqrt(jnp.mean(x * x, axis=-1, keepdims=True) + NORM_EPS) * gain


def _params(*semantics):
    return pltpu.CompilerParams(dimension_semantics=semantics, vmem_limit_bytes=VMEM_LIMIT_BYTES)


def _const_spec(shape):
    zeros = (0,) * len(shape)
    return pl.BlockSpec(shape, lambda *_: zeros, pipeline_mode=pl.Buffered(1))


def _layer_spec(stack, layer, col_block=0, cols=None):
    _, rows, n = stack.shape
    return pl.BlockSpec((None, rows, n if cols is None else cols), lambda *_: (layer, 0, col_block),
                        pipeline_mode=pl.Buffered(1))


def _hgrn_pre_kernel(layer_j, x_ref, gain_ref, w_ref, lbp_ref, q_ref, k_ref, v_ref, lf_ref, sg_ref):
    d = D_MODEL
    h = _rmsnorm_rows(x_ref[...], gain_ref[...]).astype(BF16)

    if layer_j > 0:
        p = lbp_ref[...]
        e = jnp.exp(p - jnp.max(p, axis=0, keepdims=True))
        sm = e / jnp.sum(e, axis=0, keepdims=True)
        lb = sm[1:2, :]
        for i in range(2, layer_j + 1):
            lb = lb + sm[i:i + 1, :]
        log_lb = jnp.log2(lb)
        log_1m_lb = jnp.log2(1.0 - lb)

    for j in range(d // PRE_COLS):
        out = slice(j * PRE_COLS, (j + 1) * PRE_COLS)
        proj = lambda blk: _dot(h, w_ref[:, blk * d + j * PRE_COLS:blk * d + (j + 1) * PRE_COLS].astype(BF16))

        q = proj(0)
        q_ref[:, out] = (q / (1.0 + jnp.exp2(q * (-LOG2_E)))).astype(BF16)

        f = proj(1)
        f2 = f * LOG2_E
        ef = jnp.exp2(-jnp.abs(f2))
        inv = 1.0 / (1.0 + ef)
        log_sig = jnp.minimum(f2, 0.0) - jnp.log2(1.0 + ef)
        sig_neg = jnp.where(f >= 0.0, ef * inv, inv)
        if layer_j == 0:
            lf_ref[:, out] = log_sig
            k_ref[:, out] = sig_neg.astype(BF16)
        else:
            a = log_lb[:, out]
            b = log_1m_lb[:, out] + log_sig
            lf_ref[:, out] = jnp.maximum(a, b) + jnp.log2(1.0 + jnp.exp2(-jnp.abs(a - b)))
            k_ref[:, out] = ((1.0 - lb[:, out]) * sig_neg).astype(BF16)

        v_ref[:, out] = proj(2).astype(BF16)

        g = proj(3)
        sg_ref[:, out] = (g / (1.0 + jnp.exp2(g * (-LOG2_E)))).astype(BF16)


def _hgrn_pre(x2, gain, w_in, lbp, layer_j):
    t, d = x2.shape
    tm = min(PRE_ROWS, t)
    row = lambda i: (i, 0)
    out_bf = jax.ShapeDtypeStruct((t, d), BF16)
    return pl.pallas_call(
        functools.partial(_hgrn_pre_kernel, layer_j),
        grid=(t // tm,),
        in_specs=[pl.BlockSpec((tm, d), row), _const_spec((1, d)), _layer_spec(w_in, layer_j),
                  _const_spec(lbp.shape)],
        out_specs=[pl.BlockSpec((tm, d), row)] * 5,
        out_shape=[out_bf, out_bf, out_bf, jax.ShapeDtypeStruct((t, d), F32), out_bf],
        compiler_params=_params("arbitrary"),
        name="hgrn_pre",
    )(x2, gain, w_in, lbp)


def _hgrn_levels(c):
    m, out = 1, []
    while m < c:
        out.append(m)
        m *= 2
    return out


def _hgrn_tables(c):
    levels = _hgrn_levels(c)
    idx = np.arange(c)
    x, j = idx[:, None], idx[None, :]
    blocks = []
    masks = [(x == j)]
    for m in levels:
        bnd = (x // (2 * m)) * (2 * m) + m - 1
        second = (x % (2 * m)) >= m
        if m > 1:
            blocks.append(np.where(second, (j > bnd) & (j <= x), (j > x) & (j <= bnd)))
        t, s = x, j
        mask = (t // (2 * m) == s // (2 * m)) & ((t % (2 * m)) >= m) & ((s % (2 * m)) < m)
        if m < HGRN_COMPACT_FROM:
            masks.append(mask)
        elif c // (2 * m) > 1:
            masks.append(mask[second[:, 0]])
    blocks.append(np.broadcast_to(idx[None, :] < c // 2, (HGRN_HALF_ROWS, c)))
    a = np.concatenate(blocks, axis=0).astype(np.float32)
    a2 = np.concatenate([a, a], axis=1)
    masks = np.concatenate(masks, axis=0).astype(np.float32)
    masks2 = np.concatenate([masks, masks], axis=1)
    return jnp.asarray(a2, BF16), jnp.asarray(masks2)


def _hgrn_scan_kernel(q_ref, k_ref, v_ref, lf_ref, sg_ref, a2_ref, mask_ref, onorm_ref, y_ref,
                      state_ref, g_ref):
    c = HGRN_CHUNK
    hd = HGRN_HEAD_DIM
    levels = _hgrn_levels(c)

    @pl.when(pl.program_id(1) == 0)
    def _():
        state_ref[...] = jnp.zeros_like(state_ref)

    for ci in range(HGRN_STEP_CHUNKS):
        lf_hi, lf_lo = _split_bf16(lf_ref[0, ci * c:(ci + 1) * c])
        g_ref[ci] = _dot(a2_ref[...], jnp.concatenate([lf_hi, lf_lo], axis=0))

    zeros = jnp.zeros((c, hd), BF16)
    zeros2 = jnp.zeros((c, 2 * hd), BF16)

    def block_diag(x):
        return jnp.concatenate([jnp.concatenate([x[:, :hd], zeros[:x.shape[0]]], axis=1),
                                jnp.concatenate([zeros[:x.shape[0]], x[:, hd:]], axis=1)], axis=0)

    def scaled(x_bf, log2_decay):
        return x_bf * jnp.exp2(log2_decay).astype(BF16)

    def decayed(x_bf, block, rows=slice(0, c)):
        return scaled(x_bf[rows], g_ref[ci, block * c + rows.start:block * c + rows.stop, sl])

    tile = BF16_ROWS
    half = c // 2
    for ci, p in [(ci, p) for ci in range(HGRN_STEP_CHUNKS) for p in range(HGRN_HEADS // 2)]:
        tok = slice(ci * c, (ci + 1) * c)
        sl = slice(2 * p * hd, 2 * (p + 1) * hd)
        q_bf = q_ref[0, tok, sl]
        k_bf = k_ref[0, tok, sl]
        v_bf = v_ref[0, tok, sl]
        st = [state_ref[2 * p], state_ref[2 * p + 1]]
        st_bd = block_diag(jnp.concatenate([st[0].astype(BF16), st[1].astype(BF16)], axis=1))

        n_sum = len(levels) - 1
        top = (n_sum - 1) * c
        g_lo = g_ref[ci, top:top + half, sl]
        g_hi = g_ref[ci, top + half:top + c, sl]
        first_half = g_ref[ci, n_sum * c:n_sum * c + 1, sl]
        g_last = g_hi[half - 1:half]
        since_start = jnp.concatenate([first_half - g_lo, first_half + g_hi], axis=0)
        until_end = jnp.concatenate([g_last + g_lo, g_last - g_hi], axis=0)

        o = _dot_nt(scaled(q_bf, since_start), st_bd)

        pieces = [None] * (c // tile)

        def add_rows(start, val):
            for i in range(val.shape[0] // tile):
                piece = val[i * tile:(i + 1) * tile]
                at = start // tile + i
                pieces[at] = piece if pieces[at] is None else pieces[at] + piece

        qk = q_bf.astype(F32) * k_bf.astype(F32)
        diag = jnp.concatenate(
            [jnp.broadcast_to(jnp.sum(qk[:, i * hd:(i + 1) * hd], axis=-1, keepdims=True), (c, c)) for i in range(2)],
            axis=1)
        add_rows(0, mask_ref[0:c] * diag)
        k_up = pltpu.roll(k_bf.astype(F32), 1, axis=0)
        qfk = q_bf.astype(F32) * jnp.exp2(lf_ref[0, tok, sl]) * k_up
        sub = jnp.concatenate(
            [jnp.broadcast_to(jnp.sum(qfk[:, i * hd:(i + 1) * hd], axis=-1, keepdims=True), (c, c)) for i in range(2)],
            axis=1)
        add_rows(0, mask_ref[c:2 * c] * sub)
        mrow = 2 * c
        for li, m in enumerate(levels[1:]):
            blk = li
            if m < HGRN_COMPACT_FROM:
                s = _dot_nt(decayed(q_bf, blk), block_diag(decayed(k_bf, blk)))
                add_rows(0, mask_ref[mrow:mrow + c] * s)
                mrow += c
            else:
                n_blk = c // (2 * m)
                first = [slice(b * 2 * m, b * 2 * m + m) for b in range(n_blk)]
                second = [slice(b * 2 * m + m, (b + 1) * 2 * m) for b in range(n_blk)]
                lhs = jnp.concatenate([decayed(q_bf, blk, r) for r in second], axis=0)
                keys = jnp.concatenate([x for r in first for x in (decayed(k_bf, blk, r), zeros2[:m])], axis=0)
                s = _dot_nt(lhs, block_diag(keys))
                if n_blk > 1:
                    s = mask_ref[mrow:mrow + c // 2] * s
                    mrow += c // 2
                for b, r in enumerate(second):
                    add_rows(r.start, s[b * m:(b + 1) * m])
        scores = jnp.concatenate(pieces, axis=0)
        o = o + _dot(scores.astype(BF16), block_diag(v_bf))

        d_state = _dot_tn(v_bf, scaled(k_bf, until_end))
        decay_all = jnp.exp2(first_half + g_last)
        for i in range(2):
            hs = slice(i * hd, (i + 1) * hd)
            state_ref[2 * p + i] = st[i] * decay_all[:, hs] + d_state[hs, hs]
            oh = o[:, hs]
            oh = oh * lax.rsqrt(jnp.mean(oh * oh, axis=-1, keepdims=True) + NORM_EPS * hd)
            lanes = slice((2 * p + i) * hd, (2 * p + i + 1) * hd)
            y_ref[0, tok, lanes] = (oh * onorm_ref[:, lanes] * sg_ref[0, tok, lanes].astype(F32)).astype(BF16)


def _hgrn_scan(q, k, v, lf, sg, out_norm):
    b, s, d = q.shape
    c = HGRN_CHUNK
    a2, masks = _hgrn_tables(c)
    rows = HGRN_STEP_CHUNKS * c
    assert s % rows == 0
    tok = pl.BlockSpec((1, rows, d), lambda i, j: (i, j, 0))
    return pl.pallas_call(
        _hgrn_scan_kernel,
        grid=(b, s // rows),
        in_specs=[tok, tok, tok, tok, tok, _const_spec(a2.shape), _const_spec(masks.shape),
                  _const_spec((1, d))],
        out_specs=tok,
        out_shape=jax.ShapeDtypeStruct((b, s, d), BF16),
        scratch_shapes=[pltpu.VMEM((HGRN_HEADS, HGRN_HEAD_DIM, HGRN_HEAD_DIM), F32),
                        pltpu.VMEM((HGRN_STEP_CHUNKS, a2.shape[0], d), F32)],
        compiler_params=_params("arbitrary", "arbitrary"),
        name="hgrn_scan",
    )(q, k, v, lf, sg, a2, masks, out_norm)


def _perm_block(dilation):
    return max(ATTN_BLOCK, BF16_ROWS * dilation)


def _block_perm(dilation):
    bn = _perm_block(dilation)
    w = bn // dilation
    p = np.zeros((bn, bn), np.float32)
    for r in range(dilation):
        for i in range(w):
            p[r * w + i, i * dilation + r] = 1.0
    return p


def _to_group_order(rows, perm, dilation):
    bn = _perm_block(dilation)
    w = bn // dilation
    blocks = [_dot(perm, rows[q * bn:(q + 1) * bn]).astype(rows.dtype) for q in range(ROW_TILE // bn)]
    return jnp.concatenate([blk[r * w:(r + 1) * w] for r in range(dilation) for blk in blocks], axis=0)


def _to_natural_order(rows, unperm, dilation):
    bn = _perm_block(dilation)
    w = bn // dilation
    n = ROW_TILE // dilation
    out = []
    for q in range(ROW_TILE // bn):
        gathered = jnp.concatenate([rows[r * n + q * w:r * n + (q + 1) * w] for r in range(dilation)], axis=0)
        out.append(_dot(unperm, gathered))
    return jnp.concatenate(out, axis=0)


def _span_tiles(dilation):
    return max(1, ATTN_BLOCK * dilation // ROW_TILE)


def _grouped_view(a, dilation):
    t, n = a.shape
    k = _span_tiles(dilation)
    if k == 1:
        return a
    return a.reshape(t // (k * ROW_TILE), dilation, k, ROW_TILE // dilation, n)


def _grouped_spec(n, dilation):
    k = _span_tiles(dilation)
    if k == 1:
        return pl.BlockSpec((ROW_TILE, n), lambda i, *_: (i, 0))
    return pl.BlockSpec((None, dilation, None, ROW_TILE // dilation, n), lambda i, *_: (i // k, 0, i % k, 0, 0))


def _grouped_store(ref, dilation, val):
    if _span_tiles(dilation) == 1:
        ref[...] = val
    else:
        n = ROW_TILE // dilation
        for r in range(dilation):
            ref[r] = val[r * n:(r + 1) * n]


def _grouped_load(ref, dilation):
    if _span_tiles(dilation) == 1:
        return ref[...]
    return jnp.concatenate([ref[r] for r in range(dilation)], axis=0)


def _attn_qkv_kernel(dilation, x_ref, gain_ref, *rest):
    d = D_MODEL
    h = _rmsnorm_rows(x_ref[...], gain_ref[...]).astype(BF16)
    if dilation > 1:
        perm_ref, w_ref, o_ref = rest
        h = _to_group_order(h, perm_ref[...], dilation)
    else:
        w_ref, o_ref = rest
    res = _dot(h, w_ref[...].astype(BF16))
    val = jnp.concatenate([(res[:, :d] * (ATTN_HEAD_DIM ** -0.5 * LOG2_E)).astype(BF16),
                           res[:, d:].astype(BF16)], axis=1)
    _grouped_store(o_ref, dilation, val)


def _attn_qkv(x2, gain, w, layer_j, group, dilation):
    t, d = x2.shape
    n = 3 * d
    perm = [jnp.asarray(_block_perm(dilation), BF16)] if dilation > 1 else []
    w_spec = _layer_spec(w, layer_j, group, n)
    out = pl.pallas_call(
        functools.partial(_attn_qkv_kernel, dilation),
        grid=(t // ROW_TILE,),
        in_specs=[pl.BlockSpec((ROW_TILE, d), lambda i: (i, 0)), _const_spec((1, d))]
        + [_const_spec(p.shape) for p in perm] + [w_spec],
        out_specs=_grouped_spec(n, dilation),
        out_shape=jax.ShapeDtypeStruct(_grouped_view(jnp.empty((t, n), BF16), dilation).shape, BF16),
        compiler_params=_params("arbitrary"),
        name=f"attn_qkv_d{dilation}",
    )(x2, gain, *perm, w)
    return out.reshape(t, n)


def _attn_group_kernel(dilation, blocks_per_seq, q_ref, kp_ref, kc_ref, vp_ref, vc_ref, lm_ref, sel_ref, bias_ref,
                       o_ref, lse_ref):
    blk = ATTN_BLOCK
    nb = ATTN_STEP_BLOCKS
    m_lo, m_hi = lm_ref[0], lm_ref[1]
    low_q = lax.broadcasted_iota(jnp.int32, (blk, LANES), 1) < ATTN_HEAD_DIM

    for j in range(nb):
        rows = slice(j * blk, (j + 1) * blk)
        if dilation >= nb:
            prev_k, prev_v, jp = kp_ref, vp_ref, j
        elif j >= dilation:
            prev_k, prev_v, jp = kc_ref, vc_ref, j - dilation
        else:
            prev_k, prev_v, jp = kp_ref, vp_ref, j - dilation + nb
        prev_rows = slice(jp * blk, (jp + 1) * blk)
        has_prev = (((pl.program_id(0) * nb + j) % blocks_per_seq) >= dilation).astype(jnp.int32)
        m_tile = jnp.zeros((blk, LANES), F32)
        l_tile = sel_ref[HEAD_PAIRS]

        for p in range(HEAD_PAIRS):
            sl = slice(p * LANES, (p + 1) * LANES)
            qp = q_ref[rows, sl]
            q_bd = jnp.concatenate([qp * m_lo[:blk], qp * m_hi[:blk]], axis=0)
            k_cat = jnp.concatenate([prev_k[prev_rows, sl], kc_ref[rows, sl]], axis=0)
            s = _dot_nt(q_bd, k_cat) + bias_ref[has_prev]
            s0, s1 = s[:blk], s[blk:]
            m0 = jnp.max(s0, axis=-1, keepdims=True)
            m1 = jnp.max(s1, axis=-1, keepdims=True)
            p_cat = jnp.concatenate([jnp.exp2(s0 - m0).astype(BF16), jnp.exp2(s1 - m1).astype(BF16)], axis=1)
            v_cat = jnp.concatenate([prev_v[prev_rows, sl], vc_ref[rows, sl]], axis=0)
            rhs = jnp.concatenate([jnp.concatenate([v_cat * m_lo, m_lo], axis=1),
                                   jnp.concatenate([v_cat * m_hi, m_hi], axis=1)], axis=0)
            res = _dot(p_cat, rhs)
            denom = res[:, LANES:]
            o_ref[rows, sl] = (res[:, :LANES] / denom).astype(o_ref.dtype)
            m_tile = m_tile + sel_ref[p] * jnp.where(low_q, m0, m1)
            l_tile = l_tile + sel_ref[p] * denom

        lse_ref[rows, :] = m_tile * LN_2 + jnp.log(l_tile)


def _attn_group(qkv, dilation, seq):
    t, n = qkv.shape
    d = D_MODEL
    nb = ATTN_STEP_BLOCKS
    blk = nb * ATTN_BLOCK
    assert (dilation % nb == 0 or nb % dilation == 0) and (seq // ATTN_BLOCK) % nb == 0
    back = max(dilation // nb, 1)

    def spec(which, prev):
        if prev:
            return pl.BlockSpec((blk, d), lambda i: (jnp.maximum(i - back, 0), which))
        return pl.BlockSpec((blk, d), lambda i: (i, which))

    first = np.broadcast_to(np.arange(LANES) < ATTN_HEAD_DIM, (2 * ATTN_BLOCK, LANES))
    lane_masks = jnp.asarray(np.stack([first, ~first]).astype(np.float32), BF16)
    select = np.zeros((HEAD_PAIRS + 1, ATTN_BLOCK, LANES), np.float32)
    for p in range(HEAD_PAIRS):
        select[p, :, [p, ATTN_HEAD_DIM + p]] = 1.0
    select[HEAD_PAIRS] = 1.0 - select[:HEAD_PAIRS].sum(axis=0)
    lane_select = jnp.asarray(select)
    a = np.arange(2 * ATTN_BLOCK)[:, None] % ATTN_BLOCK
    c = np.arange(2 * ATTN_BLOCK)[None, :]
    own = (c >= ATTN_BLOCK) & (c - ATTN_BLOCK <= a)
    earlier = (c < ATTN_BLOCK) & (c >= a)
    bias = jnp.asarray(np.where(np.stack([own, own | earlier]), 0.0, -np.inf).astype(np.float32))

    return pl.pallas_call(
        functools.partial(_attn_group_kernel, dilation, seq // ATTN_BLOCK),
        grid=(t // blk,),
        in_specs=[spec(0, False), spec(1, True), spec(1, False), spec(2, True), spec(2, False),
                  _const_spec(lane_masks.shape), _const_spec(lane_select.shape), _const_spec(bias.shape)],
        out_specs=[pl.BlockSpec((blk, d), lambda i: (i, 0)), pl.BlockSpec((blk, LANES), lambda i: (i, 0))],
        out_shape=[jax.ShapeDtypeStruct((t, d), BF16), jax.ShapeDtypeStruct((t, LANES), F32)],
        compiler_params=_params("arbitrary"),
        name=f"attn_group_d{dilation}",
    )(qkv, qkv, qkv, qkv, qkv, lane_masks, lane_select, bias)


def _head_expand_matrix():
    e = np.zeros((LANES, D_MODEL), np.float32)
    for h in range(ATTN_HEADS):
        lane = (h // 2) + ATTN_HEAD_DIM * (h % 2)
        e[lane, h * ATTN_HEAD_DIM:(h + 1) * ATTN_HEAD_DIM] = 1.0
    return jnp.asarray(e, BF16)


def _mlp_tail(x1, gain_ref, w1_ref, w2_ref, final_gain_ref, o_ref):
    d = D_MODEL
    h = _rmsnorm_rows(x1, gain_ref[...]).astype(BF16)
    acc = x1
    for c in range(MLP_HIDDEN // d):
        a = jnp.maximum(_dot(h, w1_ref[:, c * d:(c + 1) * d]), 0.0)
        acc = acc + _dot((a * a).astype(BF16), w2_ref[c * d:(c + 1) * d, :])
    if final_gain_ref is not None:
        acc = _rmsnorm_rows(acc, final_gain_ref[...])
    o_ref[...] = acc


def _post_kernel(final, x_ref, y_ref, wo_ref, gain_ref, w1_ref, w2_ref, *rest):
    if final:
        final_gain_ref, o_ref = rest
    else:
        final_gain_ref, (o_ref,) = None, rest
    x1 = x_ref[...] + _dot(y_ref[...], wo_ref[...])
    _mlp_tail(x1, gain_ref, w1_ref, w2_ref, final_gain_ref, o_ref)


def _attn_post_kernel(final, x_ref, o0_ref, o1_ref, o2_ref, l0_ref, l1_ref, l2_ref, u1_ref, u2_ref, exp_ref,
                      wo_ref, gain_ref, w1_ref, w2_ref, *rest):
    if final:
        final_gain_ref, o_ref = rest
    else:
        final_gain_ref, (o_ref,) = None, rest
    dils = [dil for _, dil in DILATED_GROUPS]
    outs = [o0_ref[...]]
    lses = [l0_ref[...]]
    for o_g, l_g, u_ref, dil in zip((o1_ref, o2_ref), (l1_ref, l2_ref), (u1_ref, u2_ref), dils[1:]):
        outs.append(_to_natural_order(_grouped_load(o_g, dil), u_ref[...], dil).astype(BF16))
        both = _to_natural_order(jnp.concatenate(_split_bf16(_grouped_load(l_g, dil)), axis=1), u_ref[...], dil)
        lses.append(both[:, :LANES] + both[:, LANES:])
    m = jnp.maximum(jnp.maximum(lses[0], lses[1]), lses[2])
    es = [jnp.exp(l - m) for l in lses]
    inv = 1.0 / (es[0] + es[1] + es[2])
    y = None
    for e, o_nat in zip(es, outs):
        term = _dot((e * inv).astype(BF16), exp_ref[...]) * o_nat.astype(F32)
        y = term if y is None else y + term
    x1 = x_ref[...] + _dot(y.astype(BF16), wo_ref[...])
    _mlp_tail(x1, gain_ref, w1_ref, w2_ref, final_gain_ref, o_ref)


def _post(x2, mix, wo, layer_j, gain, w1, w2, layer, final_gain):
    t, d = x2.shape
    assert t % ROW_TILE == 0
    row = lambda i: (i, 0)
    final = final_gain is not None
    weights = [wo, gain, w1, w2] + ([final_gain] if final else [])
    weight_specs = ([_layer_spec(wo, layer_j), _const_spec(gain.shape), _layer_spec(w1, layer), _layer_spec(w2, layer)]
                    + ([_const_spec(final_gain.shape)] if final else []))
    if isinstance(mix, tuple):
        dils = [dil for _, dil in DILATED_GROUPS]
        unperms = [jnp.asarray(_block_perm(dil).T, BF16) for dil in dils[1:]]
        expand = _head_expand_matrix()
        body = functools.partial(_attn_post_kernel, final)
        views = [_grouped_view(a, dil) for a, dil in zip(mix, dils + dils)]
        args = [x2, *views, *unperms, expand] + weights
        in_specs = ([pl.BlockSpec((ROW_TILE, d), row)] + [_grouped_spec(d, dil) for dil in dils]
                    + [_grouped_spec(LANES, dil) for dil in dils]
                    + [_const_spec(u.shape) for u in unperms] + [_const_spec(expand.shape)] + weight_specs)
    else:
        body = functools.partial(_post_kernel, final)
        args = [x2, mix] + weights
        in_specs = [pl.BlockSpec((ROW_TILE, d), row)] * 2 + weight_specs
    return pl.pallas_call(
        body,
        grid=(t // ROW_TILE,),
        in_specs=in_specs,
        out_specs=pl.BlockSpec((ROW_TILE, d), row),
        out_shape=jax.ShapeDtypeStruct((t, d), F32),
        compiler_params=_params("arbitrary"),
        name="post_mlp",
    )(*args)


def kernel(x, norm_mix, norm_mlp, norm_final, hgrn_w_in, hgrn_lower_bound, hgrn_out_norm, hgrn_w_out,
           attn_w_qkv, attn_w_out, mlp_w1, mlp_w2):
    b, s, d = x.shape
    depth = norm_mix.shape[0]
    n_mixers = 2
    x2 = x.reshape(b * s, d)
    row = lambda v: v.reshape(1, -1).astype(F32)
    lbp = hgrn_lower_bound.astype(F32)
    hgrn_w_out, attn_w_out, mlp_w1, mlp_w2 = (
        w.astype(BF16) for w in (hgrn_w_out, attn_w_out, mlp_w1, mlp_w2))

    for layer in range(depth):
        j = layer // n_mixers
        final_gain = row(norm_final) if layer == depth - 1 else None
        if layer % n_mixers == 0:
            q, k, v, lf, sg = _hgrn_pre(x2, row(norm_mix[layer]), hgrn_w_in, lbp, j)
            to3 = lambda a: a.reshape(b, s, d)
            mix = _hgrn_scan(to3(q), to3(k), to3(v), to3(lf), to3(sg), row(hgrn_out_norm[j])).reshape(b * s, d)
            wo = hgrn_w_out
        else:
            outs, lses = [], []
            for g, (window, dilation) in enumerate(DILATED_GROUPS):
                assert window // dilation == ATTN_BLOCK and s % (ATTN_BLOCK * dilation) == 0
                qkv = _attn_qkv(x2, row(norm_mix[layer]), attn_w_qkv, j, g, dilation)
                o_g, lse_g = _attn_group(qkv, dilation, s)
                outs.append(o_g)
                lses.append(lse_g)
            mix = (*outs, *lses)
            wo = attn_w_out
        x2 = _post(x2, mix, wo, j, row(norm_mlp[layer]), mlp_w1, mlp_w2, layer, final_gain)
    return x2.reshape(b, s, d)
```
